```python
import jax, jax.numpy as jnp
from jax import lax
import numpy as np

D_MODEL = 4096
BATCH = 16
SEQ = 256
DEPTH = 2
DEC_BATCH = 4
DEC_SEQ = 2048
PAST_LEN = 256

GRID_W = 64
Q_BLOCK = 128
ROPE_THETA = 10000.0
NORM_EPS = 1e-6
N_EVEN = (DEPTH + 1) // 2
N_ODD = DEPTH // 2

A_WIDTH = D_MODEL // 2
A_HEAD = 64
A_HEADS = A_WIDTH // A_HEAD
A_DECAY_LORA = 128
A_ICLR_LORA = 128
A_LORA_COLS = 2 * (A_DECAY_LORA + A_ICLR_LORA)
A_SHIFT = 3 * A_WIDTH + A_LORA_COLS
A_GN_EPS = 64e-5
B_HEAD = 128
B_Q_HEADS = (D_MODEL // 2) // B_HEAD
B_KV_HEADS = 4
B_GROUP = B_Q_HEADS // B_KV_HEADS
B_WIDTH = B_Q_HEADS * B_HEAD
B_KV_WIDTH = B_KV_HEADS * B_HEAD
AB_IN = A_SHIFT + A_WIDTH + 2 * B_WIDTH + 2 * B_KV_WIDTH
C_HEADS = 64
C_NOPE = 128
C_ROPE = 64
C_V = 128
C_Q_LORA = 1536
C_KV_LORA = 512
C_WIDTH = C_HEADS * C_V
C_IN = C_Q_LORA + C_KV_LORA + C_ROPE + C_WIDTH

kernel_name = "hybrid_rwkv7_gqa_mla_diffusion_step"


def rms_norm(x, g, eps=NORM_EPS):
    xf = x.astype(jnp.float32)
    y = xf * lax.rsqrt(jnp.mean(xf * xf, axis=-1, keepdims=True) + eps)
    return y.astype(x.dtype) * g


def split_cols(a, sizes):
    idx = np.cumsum(sizes)[:-1].tolist()
    return jnp.split(a, idx, axis=-1)


def centred_shift(p):
    prev = jnp.pad(p[:, :-1], ((0, 0), (1, 0), (0, 0)))
    nxt = jnp.pad(p[:, 1:], ((0, 0), (0, 1), (0, 0)))
    return 0.5 * (prev + nxt)


def axial_rope_tables(n_tokens, rot_dim, dtype):
    rows = n_tokens // GRID_W
    row = jnp.repeat(jnp.arange(rows, dtype=jnp.float32), GRID_W)
    col = jnp.tile(jnp.arange(GRID_W, dtype=jnp.float32), rows)
    n_freq = rot_dim // 4
    inv_freq = ROPE_THETA ** (-jnp.arange(n_freq, dtype=jnp.float32) / n_freq)
    ang = jnp.concatenate([row[:, None] * inv_freq, col[:, None] * inv_freq], axis=-1)
    return jnp.cos(ang).astype(dtype), jnp.sin(ang).astype(dtype)


def apply_rope(x, cos, sin):
    x2 = x.reshape(*x.shape[:-1], -1, 2)
    xr, xi = x2[..., 0], x2[..., 1]
    return jnp.stack([xr * cos - xi * sin, xr * sin + xi * cos], axis=-1).reshape(x.shape)


def blocked_attention(q_parts, score_fn, v, scale):
    Bn, Tq = q_parts[0].shape[:2]
    nb = Tq // Q_BLOCK
    qb = tuple(jnp.moveaxis(a.reshape(Bn, nb, Q_BLOCK, *a.shape[2:]), 1, 0) for a in q_parts)

    def one_block(blk):
        s = score_fn(blk).astype(jnp.float32) * scale
        p = jax.nn.softmax(s, axis=-1).astype(v.dtype)
        return jnp.einsum('bhgqk,bkhd->bqhgd', p, v)

    o = lax.map(one_block, qb)
    return jnp.moveaxis(o, 0, 1).reshape(Bn, Tq, -1)


def rwkv7_scan(s0, r, w, k, v, kk, a, reverse):
    xs = tuple(jnp.moveaxis(t, 1, 0) for t in (r, w, k, v, kk, a))

    def step(S, inp):
        r_t, w_t, k_t, v_t, kk_t, a_t = inp
        sa = jnp.einsum('bhij,bhj->bhi', S, -kk_t)
        S = (S * w_t[:, :, None, :] + sa[..., None] * (kk_t * a_t)[:, :, None, :]
             + v_t[..., None] * k_t[:, :, None, :])
        return S, jnp.einsum('bhij,bhj->bhi', S, r_t)

    s_fin, ys = lax.scan(step, s0, xs, reverse=reverse)
    return s_fin, jnp.moveaxis(ys, 0, 1)


def rwkv7_mixer(r, k, v, lora, p, s0):
    Bn, T, _ = r.shape
    f32 = jnp.float32

    def heads(t):
        return t.reshape(*t.shape[:-1], A_HEADS, A_HEAD)

    rf, kf, vf = heads(r.astype(f32)), heads(k.astype(f32)), heads(v.astype(f32))
    kk = kf * heads(p["k_k"].astype(f32))
    kk = kk / jnp.maximum(jnp.sqrt(jnp.sum(kk * kk, axis=-1, keepdims=True)), 1e-12)
    k_a = heads(p["k_a"].astype(f32))
    r_k = p["r_k"].astype(f32)
    y = 0.0
    bonus = 0.0
    finals = []
    for d in range(2):
        wl = lora[:, :, d, :A_DECAY_LORA]
        al = lora[:, :, d, A_DECAY_LORA:]
        w_pre = (p["w0"][d] + jnp.tanh(wl) @ p["w2"][d]).astype(f32)
        decay = jnp.exp(-jnp.exp(-jax.nn.softplus(-w_pre) - 0.5))
        a = heads(jax.nn.sigmoid((p["a0"][d] + al @ p["a2"][d]).astype(f32)))
        kd = kf * (1.0 + (a - 1.0) * k_a)
        init = (jnp.zeros((Bn, A_HEADS, A_HEAD, A_HEAD), f32) if s0 is None
                else s0[:, d].astype(f32))
        s_fin, y_d = rwkv7_scan(init, rf, heads(decay), kd, vf, kk, a, reverse=(d == 1))
        y = y + y_d
        bonus = bonus + jnp.sum(rf * kd * r_k, axis=-1, keepdims=True) * vf
        finals.append(s_fin)
    mean = jnp.mean(y, axis=-1, keepdims=True)
    var = jnp.mean(jnp.square(y - mean), axis=-1, keepdims=True)
    gn = ((y - mean) * lax.rsqrt(var + A_GN_EPS)).reshape(Bn, T, A_WIDTH) * p["ln_w"] + p["ln_b"]
    out = gn + bonus.reshape(Bn, T, A_WIDTH)
    return out.astype(r.dtype), jnp.stack(finals, axis=1).astype(r.dtype)


def gqa_mixer(q_b, k_b, v_b, p, ctx_kv, rope):
    Bn, T, _ = q_b.shape
    q = rms_norm(q_b.reshape(Bn, T, B_Q_HEADS, B_HEAD), p["q_g"])
    k = rms_norm(k_b.reshape(Bn, T, B_KV_HEADS, B_HEAD), p["k_g"])
    v = v_b.reshape(Bn, T, B_KV_HEADS, B_HEAD)
    if ctx_kv is None:
        keys, vals = k, v
    else:
        cos, sin = rope
        q = apply_rope(q, cos[None, :, None, :], sin[None, :, None, :])
        k = apply_rope(k, cos[None, :, None, :], sin[None, :, None, :])
        keys = jnp.concatenate([ctx_kv[0], k], axis=1)
        vals = jnp.concatenate([ctx_kv[1], v], axis=1)
    qg = q.reshape(Bn, T, B_KV_HEADS, B_GROUP, B_HEAD)
    o = blocked_attention((qg,), lambda qb: jnp.einsum('bqhgd,bkhd->bhgqk', qb[0], keys),
                          vals, B_HEAD ** -0.5)
    return o, k, v


def mixer_ab(h, p, ctx, rope):
    Bn, T, _ = h.shape
    proj = h @ p["w_in"]
    shifted, g_a, q_b, k_b, v_b, g_b = split_cols(
        proj, [A_SHIFT, A_WIDTH, B_WIDTH, B_KV_WIDTH, B_KV_WIDTH, B_WIDTH])
    shifted = shifted + p["mix"] * (centred_shift(shifted) - shifted)
    r, k, v, lora = split_cols(shifted, [A_WIDTH, A_WIDTH, A_WIDTH, A_LORA_COLS])
    lora = lora.reshape(Bn, T, 2, A_DECAY_LORA + A_ICLR_LORA)
    y_a, s_fin = rwkv7_mixer(r, k, v, lora, p, None if ctx is None else ctx[0])
    y_b, k_new, v_new = gqa_mixer(q_b, k_b, v_b, p, None if ctx is None else (ctx[1], ctx[2]), rope)
    mixed = jnp.concatenate([y_a * jax.nn.silu(g_a), y_b * jax.nn.silu(g_b)], axis=-1)
    return mixed @ p["w_out"], (s_fin, k_new, v_new)


def mixer_mla(h, p, ctx, rope):
    Bn, T, _ = h.shape
    proj = h @ p["w_in"]
    q_a, kv_a, k_pe, gate = split_cols(proj, [C_Q_LORA, C_KV_LORA, C_ROPE, C_WIDTH])
    q = (rms_norm(q_a, p["q_a_g"]) @ p["w_uq"]).reshape(Bn, T, C_HEADS, C_NOPE + C_ROPE)
    q_nope, q_pe = q[..., :C_NOPE], q[..., C_NOPE:]
    c_kv = rms_norm(kv_a, p["kv_a_g"])
    if ctx is None:
        ckv_all, kpe_all = c_kv, k_pe
    else:
        cos, sin = rope
        q_pe = apply_rope(q_pe, cos[None, :, None, :], sin[None, :, None, :])
        k_pe = apply_rope(k_pe, cos[None], sin[None])
        ckv_all = jnp.concatenate([ctx[0], c_kv], axis=1)
        kpe_all = jnp.concatenate([ctx[1], k_pe], axis=1)
    Tk = ckv_all.shape[1]
    kv = (ckv_all @ p["w_ukv"]).reshape(Bn, Tk, C_HEADS, C_NOPE + C_V)
    k_nope, v = kv[..., :C_NOPE], kv[..., C_NOPE:]

    def score_fn(qb):
        return (jnp.einsum('bqhgd,bkhd->bhgqk', qb[0], k_nope)
                + jnp.einsum('bqhgr,bkr->bhgqk', qb[1], kpe_all))

    o = blocked_attention((q_nope[:, :, :, None, :], q_pe[:, :, :, None, :]), score_fn, v,
                          (C_NOPE + C_ROPE) ** -0.5)
    return (o * jax.nn.silu(gate)) @ p["w_out"], (c_kv, k_pe)


def setup_inputs(seed: int = 0) -> dict:
    key = jax.random.key(seed)
    ks = iter(jax.random.split(key, 48))
    f32 = jnp.float32
    D = D_MODEL

    def normal(shape, scale=1.0):
        return scale * jax.random.normal(next(ks), shape, f32)

    def gain(shape):
        return 1.0 + normal(shape, 0.02)

    return {
        "x_prompt": normal((BATCH, SEQ, D)),
        "x_sample": normal((DEC_BATCH, DEC_SEQ, D)),
        "c": normal((DEC_BATCH, D)),
        "state_rwkv": normal((DEC_BATCH, N_EVEN, 2, A_HEADS, A_HEAD, A_HEAD)),
        "cache_gqa_k": normal((DEC_BATCH, N_EVEN, PAST_LEN, B_KV_HEADS, B_HEAD)),
        "cache_gqa_v": normal((DEC_BATCH, N_EVEN, PAST_LEN, B_KV_HEADS, B_HEAD)),
        "cache_mla_ckv": normal((DEC_BATCH, N_ODD, PAST_LEN, C_KV_LORA)),
        "cache_mla_kpe": normal((DEC_BATCH, N_ODD, PAST_LEN, C_ROPE)),
        "c_ctx": normal((D,)),
        "mod_w": normal((DEPTH, D, 3 * D), 0.5 * D ** -0.5),
        "mod_b": normal((DEPTH, 3 * D), 0.01),
        "norm_g": gain((DEPTH, D)),
        "ab_w_in": normal((N_EVEN, D, AB_IN), D ** -0.5),
        "ab_mix": jax.random.uniform(next(ks), (N_EVEN, A_SHIFT), f32),
        "rwkv_w0": -2.0 + normal((N_EVEN, 2, A_WIDTH), 1.0),
        "rwkv_w2": normal((N_EVEN, 2, A_DECAY_LORA, A_WIDTH), 0.1 * A_DECAY_LORA ** -0.5),
        "rwkv_a0": normal((N_EVEN, 2, A_WIDTH), 0.5),
        "rwkv_a2": normal((N_EVEN, 2, A_ICLR_LORA, A_WIDTH), 0.1 * A_ICLR_LORA ** -0.5),
        "rwkv_k_k": 0.85 + normal((N_EVEN, A_WIDTH), 0.02),
        "rwkv_k_a": gain((N_EVEN, A_WIDTH)),
        "rwkv_r_k": normal((N_EVEN, A_HEADS, A_HEAD), 0.1),
        "rwkv_ln_w": gain((N_EVEN, A_WIDTH)),
        "rwkv_ln_b": normal((N_EVEN, A_WIDTH), 0.01),
        "gqa_q_g": gain((N_EVEN, B_HEAD)),
        "gqa_k_g": gain((N_EVEN, B_HEAD)),
        "ab_w_out": normal((N_EVEN, A_WIDTH + B_WIDTH, D), (A_WIDTH + B_WIDTH) ** -0.5),
        "mla_w_in": normal((N_ODD, D, C_IN), D ** -0.5),
        "mla_q_a_g": gain((N_ODD, C_Q_LORA)),
        "mla_kv_a_g": gain((N_ODD, C_KV_LORA)),
        "mla_w_uq": normal((N_ODD, C_Q_LORA, C_HEADS * (C_NOPE + C_ROPE)), C_Q_LORA ** -0.5),
        "mla_w_ukv": normal((N_ODD, C_KV_LORA, C_HEADS * (C_NOPE + C_V)), C_KV_LORA ** -0.5),
        "mla_w_out": normal((N_ODD, C_WIDTH, D), C_WIDTH ** -0.5),
        "final_g": gain((D,)),
    }


def reference(x_prompt, x_sample, c, state_rwkv, cache_gqa_k, cache_gqa_v, cache_mla_ckv,
              cache_mla_kpe, c_ctx, mod_w, mod_b, norm_g, ab_w_in, ab_mix, rwkv_w0, rwkv_w2,
              rwkv_a0, rwkv_a2, rwkv_k_k, rwkv_k_a, rwkv_r_k, rwkv_ln_w, rwkv_ln_b, gqa_q_g,
              gqa_k_g, ab_w_out, mla_w_in, mla_q_a_g, mla_kv_a_g, mla_w_uq, mla_w_ukv,
              mla_w_out, final_g):
    def ab_params(j):
        return {"w_in": ab_w_in[j], "mix": ab_mix[j], "w0": rwkv_w0[j], "w2": rwkv_w2[j],
                "a0": rwkv_a0[j], "a2": rwkv_a2[j], "k_k": rwkv_k_k[j], "k_a": rwkv_k_a[j],
                "r_k": rwkv_r_k[j], "ln_w": rwkv_ln_w[j], "ln_b": rwkv_ln_b[j],
                "q_g": gqa_q_g[j], "k_g": gqa_k_g[j], "w_out": ab_w_out[j]}

    def mla_params(j):
        return {"w_in": mla_w_in[j], "q_a_g": mla_q_a_g[j], "kv_a_g": mla_kv_a_g[j],
                "w_uq": mla_w_uq[j], "w_ukv": mla_w_ukv[j], "w_out": mla_w_out[j]}

    def run_trunk(x, cond, caches):
        T = x.shape[1]
        rope_b = rope_c = None
        if caches is not None:
            rope_b = axial_rope_tables(T, B_HEAD, x.dtype)
            rope_c = axial_rope_tables(T, C_ROPE, x.dtype)
        s_l, gk_l, gv_l, ckv_l, kpe_l = [], [], [], [], []
        for layer in range(DEPTH):
            mod = jax.nn.silu(cond) @ mod_w[layer] + mod_b[layer]
            shift, scale, gate = jnp.split(mod[:, None, :], 3, axis=-1)
            h = rms_norm(x, norm_g[layer]) * (1.0 + scale) + shift
            j = layer // 2
            if layer % 2 == 0:
                ctx = None if caches is None else (caches[0][:, j], caches[1][:, j], caches[2][:, j])
                out, (s_new, k_new, v_new) = mixer_ab(h, ab_params(j), ctx, rope_b)
                if caches is None:
                    s_l.append(s_new)
                    gk_l.append(k_new)
                    gv_l.append(v_new)
            else:
                ctx = None if caches is None else (caches[3][:, j], caches[4][:, j])
                out, (ckv_new, kpe_new) = mixer_mla(h, mla_params(j), ctx, rope_c)
                if caches is None:
                    ckv_l.append(ckv_new)
                    kpe_l.append(kpe_new)
            x = x + gate * out
        y = rms_norm(x, final_g)
        if caches is None:
            return y, (jnp.stack(s_l, 1), jnp.stack(gk_l, 1), jnp.stack(gv_l, 1),
                       jnp.stack(ckv_l, 1), jnp.stack(kpe_l, 1))
        return y, None

    y_prompt, ctx_out = run_trunk(x_prompt, c_ctx[None, :], None)
    y_sample, _ = run_trunk(x_sample, c, (state_rwkv, cache_gqa_k, cache_gqa_v,
                                          cache_mla_ckv, cache_mla_kpe))
    new_state_rwkv, new_cache_gqa_k, new_cache_gqa_v, new_cache_mla_ckv, new_cache_mla_kpe = ctx_out
    return (y_prompt, y_sample, new_state_rwkv, new_cache_gqa_k, new_cache_gqa_v,
            new_cache_mla_ckv, new_cache_mla_kpe)
```

```python
import functools
from typing import NamedTuple

import jax
import jax.numpy as jnp
import numpy as np
from jax import lax
from jax.experimental import pallas as pl
from jax.experimental.pallas import tpu as pltpu

F32 = jnp.float32
BF16 = jnp.bfloat16

LANES = 128
SUBLANES = 8
MXU_DIM = 256
VMEM_LIMIT_BYTES = 56 * 1024 * 1024

NORM_EPS = 1e-6
GN_EPS = 64e-5
ROPE_THETA = 10000.0
GRID_W = 64
HEAD_A = 64
HEAD_B = 128
B_KV_HEADS = 4
C_NOPE = 128
C_ROPE = 64
C_V = 128
C_Q_LORA = 1536
C_KV_LORA = 512
LORA = 128
RW_CHUNK = 64

NN = (((1,), (0,)), ((), ()))
NT = (((1,), (1,)), ((), ()))


class Cond(NamedTuple):
    base: int
    per_batch: int
    seq: int

    def row(self, i, tile_rows):
        return self.base + (i * tile_rows // self.seq) * self.per_batch


def _cparams(*sem):
    return pltpu.CompilerParams(dimension_semantics=sem, vmem_limit_bytes=VMEM_LIMIT_BYTES)


def _sigmoid(x):
    return 1.0 / (1.0 + jnp.exp(-x))


def _split(x):
    hi = x.astype(BF16)
    lo = (x - hi.astype(F32)).astype(BF16)
    return hi, lo


def _dot(a, b, dims=NN, passes=1):
    if passes == 1:
        return lax.dot_general(a.astype(BF16), b.astype(BF16), dims, preferred_element_type=F32)
    ah, al = _split(a)
    bh, bl = _split(b)
    out = lax.dot_general(ah, bh, dims, preferred_element_type=F32)
    out = out + lax.dot_general(ah, bl, dims, preferred_element_type=F32)
    return out + lax.dot_general(al, bh, dims, preferred_element_type=F32)


def _group_sum(x, ones_bd):
    hi, lo = _split(x)
    outs = []
    for g in range(x.shape[1] // MXU_DIM):
        sl = slice(g * MXU_DIM, (g + 1) * MXU_DIM)
        outs.append(jnp.dot(hi[:, sl], ones_bd, preferred_element_type=F32)
                    + jnp.dot(lo[:, sl], ones_bd, preferred_element_type=F32))
    return outs[0] if len(outs) == 1 else jnp.concatenate(outs, axis=-1)


def _ones_block_diag(block):
    idx = np.arange(MXU_DIM) // block
    return jnp.asarray(idx[:, None] == idx[None, :], dtype=BF16)


def _rope(x, cos_e, sin_e):
    even = lax.broadcasted_iota(jnp.int32, cos_e.shape, 1) % 2 == 0
    outs = []
    for h in range(x.shape[1] // LANES):
        xh = x[:, h * LANES:(h + 1) * LANES]
        partner = jnp.where(even, pltpu.roll(xh, LANES - 1, 1), pltpu.roll(xh, 1, 1))
        outs.append(xh * cos_e + partner * sin_e)
    return outs[0] if len(outs) == 1 else jnp.concatenate(outs, axis=-1)


def _rope_tables(n_tokens, rot_dim):
    rows = n_tokens // GRID_W
    row = jnp.repeat(jnp.arange(rows, dtype=F32), GRID_W)
    col = jnp.tile(jnp.arange(GRID_W, dtype=F32), rows)
    n_freq = rot_dim // 4
    inv_freq = ROPE_THETA ** (-jnp.arange(n_freq, dtype=F32) / n_freq)
    ang = jnp.concatenate([row[:, None] * inv_freq, col[:, None] * inv_freq], axis=-1)
    cos = jnp.repeat(jnp.cos(ang), 2, axis=-1)
    sin = jnp.repeat(jnp.sin(ang), 2, axis=-1) * jnp.tile(jnp.asarray([-1.0, 1.0], F32), rot_dim // 2)
    reps = LANES // rot_dim
    return jnp.tile(cos, (1, reps)), jnp.tile(sin, (1, reps))


def _mod_kernel(c_ref, w_ref, b_ref, o_ref):
    c = c_ref[...]
    s = c * _sigmoid(c)
    o_ref[0] = jnp.dot(s.astype(BF16), w_ref[0].astype(BF16), preferred_element_type=F32) + b_ref[0]


def modulation(cond, mod_w, mod_b, tn=512):
    depth, d, n = mod_w.shape
    rows = cond.shape[0]
    return pl.pallas_call(
        _mod_kernel,
        grid=(depth, n // tn),
        in_specs=[pl.BlockSpec((rows, d), lambda l, j: (0, 0)),
                  pl.BlockSpec((1, d, tn), lambda l, j: (l, 0, j)),
                  pl.BlockSpec((1, 1, tn), lambda l, j: (l, 0, j))],
        out_specs=pl.BlockSpec((1, rows, tn), lambda l, j: (l, 0, j)),
        out_shape=jax.ShapeDtypeStruct((depth, rows, n), F32),
        compiler_params=_cparams("parallel", "parallel"),
        name="modulation",
    )(cond, mod_w, mod_b.reshape(depth, 1, n))


def _norm_kernel(x_ref, g_ref, *rest, mod):
    x = x_ref[...]
    y = x * lax.rsqrt(jnp.mean(x * x, axis=-1, keepdims=True) + NORM_EPS) * g_ref[...]
    if mod:
        sc_ref, sh_ref, o_ref = rest
        y = y * (1.0 + sc_ref[0]) + sh_ref[0]
    else:
        (o_ref,) = rest
    o_ref[...] = y.astype(o_ref.dtype)


def rms_norm(x, g, out_dtype, scale=None, shift=None, cond=None, tr=256):
    n, d = x.shape
    row = pl.BlockSpec((tr, d), lambda i: (i, 0))
    specs = [row, pl.BlockSpec((1, d), lambda i: (0, 0))]
    args = [x, g.reshape(1, d)]
    if cond is not None:
        cspec = pl.BlockSpec((1, 1, d), lambda i: (cond.row(i, tr), 0, 0))
        specs += [cspec, cspec]
        args += [scale, shift]
    return pl.pallas_call(
        functools.partial(_norm_kernel, mod=cond is not None),
        grid=(n // tr,),
        in_specs=specs,
        out_specs=row,
        out_shape=jax.ShapeDtypeStruct((n, d), out_dtype),
        compiler_params=_cparams("parallel"),
        name="rms_norm",
    )(*args)


def _mm_kernel(*refs, n_pairs, resid, out_scale):
    acc = None
    for p in range(n_pairs):
        part = jnp.dot(refs[p][...], refs[n_pairs + p][...], preferred_element_type=F32)
        acc = part if acc is None else acc + part
    if out_scale != 1.0:
        acc = acc * out_scale
    if resid:
        x_ref, gate_ref, o_ref = refs[2 * n_pairs:]
        acc = x_ref[...] + gate_ref[0] * acc
    else:
        (o_ref,) = refs[2 * n_pairs:]
    o_ref[...] = acc.astype(o_ref.dtype)


def matmul(a_list, w_list, out_dtype, tm, tn, resid=None, gate=None, cond=None, out_scale=1.0):
    m = a_list[0].shape[0]
    n = w_list[0].shape[1]
    specs = [pl.BlockSpec((tm, a.shape[1]), lambda i, j: (i, 0)) for a in a_list]
    specs += [pl.BlockSpec((w.shape[0], tn), lambda i, j: (0, j)) for w in w_list]
    args = list(a_list) + list(w_list)
    tile = pl.BlockSpec((tm, tn), lambda i, j: (i, j))
    if resid is not None:
        specs += [tile, pl.BlockSpec((1, 1, tn), lambda i, j: (cond.row(i, tm), 0, j))]
        args += [resid, gate]
    return pl.pallas_call(
        functools.partial(_mm_kernel, n_pairs=len(a_list), resid=resid is not None, out_scale=out_scale),
        grid=(m // tm, n // tn),
        in_specs=specs,
        out_specs=tile,
        out_shape=jax.ShapeDtypeStruct((m, n), out_dtype),
        compiler_params=_cparams("parallel", "parallel"),
        name="matmul",
    )(*args)


def _rwkv_prep_kernel(x_ref, xp_ref, xn_ref, mix_ref, w0_ref, w2_ref, a0_ref, a2_ref, kk_ref, ka_ref,
                      rk_ref, ones_ref, r_out, v_out, kk_out, b_out, kd_out, lw_out, bonus_out, *,
                      tiles_per_seq, width):
    i = pl.program_id(0)
    x = x_ref[...]
    tr = x.shape[0]
    pos = i % tiles_per_seq
    prev_row = jnp.where(pos == 0, 0.0, xp_ref[SUBLANES - 1:SUBLANES, :])
    next_row = jnp.where(pos == tiles_per_seq - 1, 0.0, xn_ref[0:1, :])
    rid = lax.broadcasted_iota(jnp.int32, (tr, 1), 0)
    prev = jnp.where(rid == 0, prev_row, pltpu.roll(x, 1, 0))
    nxt = jnp.where(rid == tr - 1, next_row, pltpu.roll(x, tr - 1, 0))
    s = x + mix_ref[...] * (0.5 * (prev + nxt) - x)
    r = s[:, :width]
    k = s[:, width:2 * width]
    v = s[:, 2 * width:3 * width]
    ones_bd = ones_ref[...]
    kkr = k * kk_ref[...]
    kk = kkr / jnp.maximum(jnp.sqrt(_group_sum(kkr * kkr, ones_bd)), 1e-12)
    r_out[...] = r
    v_out[...] = v
    kk_out[...] = kk
    rk = r * rk_ref[...]
    bonus = None
    for d in range(2):
        base = 3 * width + 2 * LORA * d
        wl = s[:, base:base + LORA]
        al = s[:, base + LORA:base + 2 * LORA]
        w_pre = w0_ref[d] + jnp.dot(jnp.tanh(wl).astype(BF16), w2_ref[d], preferred_element_type=F32)
        lw_out[d] = -jnp.exp(-0.5) * _sigmoid(w_pre)
        a = _sigmoid(a0_ref[d] + jnp.dot(al.astype(BF16), a2_ref[d], preferred_element_type=F32))
        kd = k * (1.0 + (a - 1.0) * ka_ref[...])
        kd_out[d] = kd
        b_out[d] = kk * a
        bd = _group_sum(rk * kd, ones_bd)
        bonus = bd if bonus is None else bonus + bd
    bonus_out[...] = bonus * v


def rwkv_prep(shifted, p, *, seq, tr=128):
    n, cols = shifted.shape
    width = p["k_k"].shape[-1]
    nb = n // SUBLANES
    per_tile = tr // SUBLANES
    row = lambda c: pl.BlockSpec((1, c), lambda i: (0, 0))
    d3 = lambda a, b: pl.BlockSpec((2, a, b), lambda i: (0, 0, 0))
    shared = pl.BlockSpec((tr, width), lambda i: (i, 0))
    perdir = pl.BlockSpec((2, tr, width), lambda i: (0, i, 0))
    f32 = lambda shape: jax.ShapeDtypeStruct(shape, F32)
    return pl.pallas_call(
        functools.partial(_rwkv_prep_kernel, tiles_per_seq=seq // tr, width=width),
        grid=(n // tr,),
        in_specs=[pl.BlockSpec((tr, cols), lambda i: (i, 0)),
                  pl.BlockSpec((SUBLANES, cols), lambda i: (jnp.maximum(i * per_tile - 1, 0), 0)),
                  pl.BlockSpec((SUBLANES, cols), lambda i: (jnp.minimum((i + 1) * per_tile, nb - 1), 0)),
                  row(cols), d3(1, width), d3(LORA, width), d3(1, width), d3(LORA, width),
                  row(width), row(width), row(width),
                  pl.BlockSpec((MXU_DIM, MXU_DIM), lambda i: (0, 0))],
        out_specs=[shared, shared, shared, perdir, perdir, perdir, shared],
        out_shape=[f32((n, width))] * 3 + [f32((2, n, width))] * 3 + [f32((n, width))],
        compiler_params=_cparams("parallel"),
        name="rwkv_prep",
    )(shifted, shifted, shifted, p["mix"].reshape(1, cols), p["w0"].reshape(2, 1, width),
      p["w2"].astype(BF16), p["a0"].reshape(2, 1, width), p["a2"].astype(BF16),
      p["k_k"].reshape(1, width), p["k_a"].reshape(1, width), p["r_k"].reshape(1, width),
      _ones_block_diag(HEAD_A))


def _rwkv_pair_chunk(r, v, kk, b, kd, lw, H, tri, strict, incl, eye, p0):
    C = RW_CHUNK
    lw_hi, lw_lo = _split(lw)
    lw_lo2 = (lw - lw_hi.astype(F32) - lw_lo.astype(F32)).astype(BF16)
    L = (jnp.dot(tri, lw_hi, preferred_element_type=F32)
         + jnp.dot(tri, lw_lo, preferred_element_type=F32)
         + jnp.dot(tri, lw_lo2, preferred_element_type=F32))
    tot = jnp.sum(lw, axis=0, keepdims=True)
    g = jnp.exp(L)
    gp = jnp.exp(L - lw)
    gi = jnp.exp(-L)
    er = jnp.exp(tot - L)
    g_c = jnp.exp(tot)
    av = -kk * gp
    rt = r * g
    bt = b * gi
    kt = kd * gi
    bh = b * er
    kh = kd * er

    def stack(x):
        return jnp.concatenate([jnp.where(p0, x, 0.0), jnp.where(p0, 0.0, x)], axis=0)

    AV = stack(av)
    RT = stack(rt)
    VV = stack(v)
    lhs1 = jnp.concatenate([AV, RT], axis=0)
    rhs1 = jnp.concatenate([bt, bt, kt, kt], axis=0)
    o1 = _dot(lhs1, rhs1, NT)
    c2 = 2 * C
    a_ab = jnp.where(strict, o1[:c2, :c2], 0.0)
    a_ak = jnp.where(strict, o1[:c2, c2:], 0.0)
    r_b = jnp.where(incl, o1[c2:, :c2], 0.0)
    r_k = jnp.where(incl, o1[c2:, c2:], 0.0)
    T = eye + a_ab
    P = a_ab
    for _ in range(C.bit_length() - 2):
        P = _dot(P, P)
        T = T + _dot(T, P)
    z1 = _dot(a_ak, VV)
    au = _dot(T, jnp.concatenate([AV, z1], axis=1))
    Z = jnp.concatenate([au, jnp.concatenate([jnp.zeros_like(VV), VV], axis=1)], axis=0)
    ry = _dot(jnp.concatenate([r_b, r_k], axis=1), Z)
    rp = RT + ry[:, :c2]
    y0 = ry[:, c2:]
    W = jnp.concatenate([stack(bh), stack(kh)], axis=0)
    mn = _dot(W.T, Z)
    M = jnp.where(eye > 0.5, g_c, 0.0) + mn[:, :c2]
    n0 = mn[:, c2:]
    Y = _dot(rp, H) + y0
    y = Y[:C] + Y[C:]
    Hn = _dot(M, H) + n0
    return y, Hn


def _rwkv_kernel(r_ref, v_ref, kk_ref, b_ref, kd_ref, lw_ref, s0_ref, y_ref, st_ref, h_scr, *, G, TB):
    C = RW_CHUNK
    d = pl.program_id(0)
    t = pl.program_id(3)
    nt = pl.num_programs(3)
    nc = TB // C
    rev = d == 1

    @pl.when(t == 0)
    def _():
        h_scr[...] = s0_ref[0, 0]

    c2 = 2 * C
    row = lax.broadcasted_iota(jnp.int32, (c2, c2), 0)
    col = lax.broadcasted_iota(jnp.int32, (c2, c2), 1)
    same = (row >= C) == (col >= C)
    diff = row - col
    diff = jnp.where(rev, -diff, diff)
    diff = jnp.where(same, diff, -1)
    strict = diff > 0
    incl = diff >= 0
    eye = (row == col).astype(F32)
    tr = lax.broadcasted_iota(jnp.int32, (C, C), 0) - lax.broadcasted_iota(jnp.int32, (C, C), 1)
    tr = jnp.where(rev, -tr, tr)
    tri = (tr >= 0).astype(BF16)
    p0 = lax.broadcasted_iota(jnp.int32, (C, LANES), 1) < HEAD_A

    def body(ci, carry):
        ce = jnp.where(rev, nc - 1 - ci, ci)
        rows = pl.ds(pl.multiple_of(ce * C, C), C)
        for p in range(G):
            ls = slice(p * LANES, (p + 1) * LANES)
            y, hn = _rwkv_pair_chunk(
                r_ref[rows, ls], v_ref[rows, ls], kk_ref[rows, ls], b_ref[0, rows, ls],
                kd_ref[0, rows, ls], lw_ref[0, rows, ls], h_scr[p], tri, strict, incl, eye, p0)
            y_ref[0, rows, ls] = y
            h_scr[p] = hn
        return carry

    lax.fori_loop(0, nc, body, 0)

    @pl.when(t == nt - 1)
    def _():
        st_ref[0, 0] = h_scr[...]


def rwkv_scan(r, v, kk, b, kd, lw, s0, *, batch, seq, G=2, TB=256):
    n, width = r.shape
    assert n == batch * seq and seq % TB == 0 and TB % RW_CHUNK == 0
    pairs = width // LANES
    assert pairs % G == 0
    nt = seq // TB

    def tmap(dd, bb, tt):
        return bb * nt + tt + dd * (nt - 1 - 2 * tt)

    shared = pl.BlockSpec((TB, G * LANES), lambda dd, bb, gg, tt: (tmap(dd, bb, tt), gg))
    perdir = pl.BlockSpec((1, TB, G * LANES), lambda dd, bb, gg, tt: (dd, tmap(dd, bb, tt), gg))
    state = pl.BlockSpec((1, 1, G, LANES, LANES), lambda dd, bb, gg, tt: (bb, dd, gg, 0, 0))
    return pl.pallas_call(
        functools.partial(_rwkv_kernel, G=G, TB=TB),
        grid=(2, batch, pairs // G, nt),
        in_specs=[shared, shared, shared, perdir, perdir, perdir, state],
        out_specs=[perdir, state],
        out_shape=[jax.ShapeDtypeStruct((2, n, width), F32),
                   jax.ShapeDtypeStruct(s0.shape, F32)],
        scratch_shapes=[pltpu.VMEM((G, LANES, LANES), F32)],
        compiler_params=_cparams("parallel", "parallel", "parallel", "arbitrary"),
        name="rwkv_scan",
    )(r, v, kk, b, kd, lw, s0)


def _state_to_block_diag(s):
    bsz, _, heads, n, _ = s.shape
    st = jnp.swapaxes(s, -1, -2).reshape(bsz, 2, heads // 2, 2, n, n)
    z = jnp.zeros_like(st[:, :, :, 0])
    top = jnp.concatenate([st[:, :, :, 0], z], axis=-1)
    bot = jnp.concatenate([z, st[:, :, :, 1]], axis=-1)
    return jnp.concatenate([top, bot], axis=-2)


def _state_from_block_diag(sbd):
    bsz, _, pairs, _, _ = sbd.shape
    n = HEAD_A
    blocks = jnp.stack([sbd[:, :, :, :n, :n], sbd[:, :, :, n:, n:]], axis=3)
    return jnp.swapaxes(blocks.reshape(bsz, 2, 2 * pairs, n, n), -1, -2)


def _rwkv_post_kernel(y_ref, bonus_ref, gate_ref, lnw_ref, lnb_ref, ones_ref, o_ref):
    ones_bd = ones_ref[...]
    y = y_ref[0] + y_ref[1]
    mean = _group_sum(y, ones_bd) * (1.0 / HEAD_A)
    yc = y - mean
    var = _group_sum(yc * yc, ones_bd) * (1.0 / HEAD_A)
    out = yc * lax.rsqrt(var + GN_EPS) * lnw_ref[...] + lnb_ref[...] + bonus_ref[...]
    gt = gate_ref[...].astype(F32)
    o_ref[...] = (out * gt * _sigmoid(gt)).astype(o_ref.dtype)


def rwkv_post(y, bonus, gate, ln_w, ln_b, tr=256):
    _, n, width = y.shape
    tile = pl.BlockSpec((tr, width), lambda i: (i, 0))
    row = pl.BlockSpec((1, width), lambda i: (0, 0))
    return pl.pallas_call(
        _rwkv_post_kernel,
        grid=(n // tr,),
        in_specs=[pl.BlockSpec((2, tr, width), lambda i: (0, i, 0)), tile, tile, row, row,
                  pl.BlockSpec((MXU_DIM, MXU_DIM), lambda i: (0, 0))],
        out_specs=tile,
        out_shape=jax.ShapeDtypeStruct((n, width), BF16),
        compiler_params=_cparams("parallel"),
        name="rwkv_post",
    )(y, bonus, gate, ln_w.reshape(1, width), ln_b.reshape(1, width), _ones_block_diag(HEAD_A))


def _gqa_prep_kernel(q_ref, k_ref, v_ref, qg_ref, kg_ref, ones_ref, *rest, rope, scale):
    if rope:
        cos_ref, sin_ref, qo_ref, kf_ref, kb_ref, vb_ref = rest
    else:
        qo_ref, kf_ref, kb_ref, vb_ref = rest
    ones_bd = ones_ref[...]

    def norm(x, g):
        ms = _group_sum(x * x, ones_bd) * (1.0 / HEAD_B)
        return x * lax.rsqrt(ms + NORM_EPS) * g

    q = norm(q_ref[...], qg_ref[...])
    k = norm(k_ref[...], kg_ref[...])
    if rope:
        q = _rope(q, cos_ref[...], sin_ref[...])
        k = _rope(k, cos_ref[...], sin_ref[...])
    qo_ref[...] = (q * scale).astype(BF16)
    kf_ref[...] = k
    kb_ref[...] = k.astype(BF16)
    vb_ref[...] = v_ref[...].astype(BF16)


def gqa_prep(q, k, v, q_g, k_g, *, seq, rope, tr=256):
    n, qw = q.shape
    kw = k.shape[1]
    tps = seq // tr
    qt = pl.BlockSpec((tr, qw), lambda i: (i, 0))
    kt = pl.BlockSpec((tr, kw), lambda i: (i, 0))
    specs = [qt, kt, kt, pl.BlockSpec((1, qw), lambda i: (0, 0)), pl.BlockSpec((1, kw), lambda i: (0, 0)),
             pl.BlockSpec((MXU_DIM, MXU_DIM), lambda i: (0, 0))]
    args = [q, k, v, jnp.tile(q_g, qw // HEAD_B).reshape(1, qw), jnp.tile(k_g, kw // HEAD_B).reshape(1, kw),
            _ones_block_diag(HEAD_B)]
    if rope:
        tab = pl.BlockSpec((tr, LANES), lambda i: (i % tps, 0))
        specs += [tab, tab]
        args += list(_rope_tables(seq, HEAD_B))
    return pl.pallas_call(
        functools.partial(_gqa_prep_kernel, rope=rope, scale=HEAD_B ** -0.5),
        grid=(n // tr,),
        in_specs=specs,
        out_specs=[qt, kt, kt, kt],
        out_shape=[jax.ShapeDtypeStruct((n, qw), BF16), jax.ShapeDtypeStruct((n, kw), F32),
                   jax.ShapeDtypeStruct((n, kw), BF16), jax.ShapeDtypeStruct((n, kw), BF16)],
        compiler_params=_cparams("parallel"),
        name="gqa_prep",
    )(*args)


def _softmax_pv(s, v):
    m = jnp.max(s, axis=-1, keepdims=True)
    p = jnp.exp(s - m)
    l = jnp.sum(p, axis=-1, keepdims=True)
    return jnp.dot(p.astype(BF16), v, preferred_element_type=F32) / l


def _gqa_attn_kernel(q_ref, k_ref, v_ref, g_ref, o_ref, *, group):
    q = q_ref[...]
    tq = q.shape[0]
    q4 = jnp.concatenate([q[:, g * HEAD_B:(g + 1) * HEAD_B] for g in range(group)], axis=0)
    s = lax.dot_general(q4, k_ref[0], NT, preferred_element_type=F32)
    o = _softmax_pv(s, v_ref[0])
    o = jnp.concatenate([o[g * tq:(g + 1) * tq] for g in range(group)], axis=-1)
    gt = g_ref[...].astype(F32)
    o_ref[...] = (o * gt * _sigmoid(gt)).astype(o_ref.dtype)


def gqa_attention(q, keys, vals, gate, *, batch, seq, tq):
    n, qw = q.shape
    tk = keys.shape[1]
    group = qw // HEAD_B // B_KV_HEADS
    gw = group * HEAD_B
    tps = seq // tq
    qt = pl.BlockSpec((tq, gw), lambda b, h, i: (b * tps + i, h))
    kvt = pl.BlockSpec((1, tk, HEAD_B), lambda b, h, i: (b, 0, h))
    return pl.pallas_call(
        functools.partial(_gqa_attn_kernel, group=group),
        grid=(batch, B_KV_HEADS, tps),
        in_specs=[qt, kvt, kvt, qt],
        out_specs=qt,
        out_shape=jax.ShapeDtypeStruct((n, qw), BF16),
        compiler_params=_cparams("parallel", "parallel", "parallel"),
        name="gqa_attention",
    )(q, keys, vals, gate)


def _mla_prep_kernel(qkv_ref, kpe_ref, qg_ref, kvg_ref, *rest, rope):
    if rope:
        cos_ref, sin_ref, qa_ref, ckv_f_ref, ckv_b_ref, kpe_b_ref = rest
    else:
        qa_ref, ckv_f_ref, ckv_b_ref, kpe_b_ref = rest

    def norm(x, g):
        return x * lax.rsqrt(jnp.mean(x * x, axis=-1, keepdims=True) + NORM_EPS) * g

    x = qkv_ref[...]
    qa_ref[...] = norm(x[:, :C_Q_LORA], qg_ref[...]).astype(BF16)
    ckv = norm(x[:, C_Q_LORA:], kvg_ref[...])
    ckv_f_ref[...] = ckv
    ckv_b_ref[...] = ckv.astype(BF16)
    kpe = kpe_ref[...]
    if rope:
        kpe = _rope(kpe, cos_ref[...], sin_ref[...])
    kpe_b_ref[...] = (kpe + pltpu.roll(kpe, C_ROPE, 1)).astype(BF16)


def mla_prep(qkv_a, kpe, q_g, kv_g, *, seq, rope, tr=256):
    n, w = qkv_a.shape
    tps = seq // tr
    tile = lambda c: pl.BlockSpec((tr, c), lambda i: (i, 0))
    row = lambda c: pl.BlockSpec((1, c), lambda i: (0, 0))
    specs = [tile(w), tile(LANES), row(C_Q_LORA), row(C_KV_LORA)]
    args = [qkv_a, kpe, q_g.reshape(1, C_Q_LORA), kv_g.reshape(1, C_KV_LORA)]
    if rope:
        tab = pl.BlockSpec((tr, LANES), lambda i: (i % tps, 0))
        specs += [tab, tab]
        args += list(_rope_tables(seq, C_ROPE))
    return pl.pallas_call(
        functools.partial(_mla_prep_kernel, rope=rope),
        grid=(n // tr,),
        in_specs=specs,
        out_specs=[tile(C_Q_LORA), tile(C_KV_LORA), tile(C_KV_LORA), tile(LANES)],
        out_shape=[jax.ShapeDtypeStruct((n, C_Q_LORA), BF16), jax.ShapeDtypeStruct((n, C_KV_LORA), F32),
                   jax.ShapeDtypeStruct((n, C_KV_LORA), BF16), jax.ShapeDtypeStruct((n, LANES), BF16)],
        compiler_params=_cparams("parallel"),
        name="mla_prep",
    )(*args)


def _mla_attn_kernel(qn_ref, qp_ref, kv_ref, kp_ref, g_ref, *rest, rope):
    if rope:
        cos_ref, sin_ref, o_ref = rest
    else:
        (o_ref,) = rest
    qp = qp_ref[...]
    if rope:
        qp = _rope(qp, cos_ref[...], sin_ref[...])
    first = lax.broadcasted_iota(jnp.int32, qp.shape, 1) < C_ROPE
    kp = kp_ref[0]
    hw = C_NOPE + C_V
    outs = []
    for j in range(2):
        qpj = jnp.where(first if j == 0 else jnp.logical_not(first), qp, 0.0).astype(BF16)
        qj = jnp.concatenate([qn_ref[:, j * C_NOPE:(j + 1) * C_NOPE], qpj], axis=-1)
        kj = jnp.concatenate([kv_ref[0, :, j * hw:j * hw + C_NOPE], kp], axis=-1)
        s = lax.dot_general(qj, kj, NT, preferred_element_type=F32)
        outs.append(_softmax_pv(s, kv_ref[0, :, j * hw + C_NOPE:(j + 1) * hw]))
    o = jnp.concatenate(outs, axis=-1)
    gt = g_ref[...].astype(F32)
    o_ref[...] = (o * gt * _sigmoid(gt)).astype(o_ref.dtype)


def mla_attention(q_nope, q_pe, kv, kpe, gate, *, batch, seq, tq, rope):
    n, w = q_nope.shape
    tk = kv.shape[1]
    pairs = w // (2 * C_NOPE)
    tps = seq // tq
    hw = C_NOPE + C_V
    qt = pl.BlockSpec((tq, 2 * C_NOPE), lambda b, h, i: (b * tps + i, h))
    specs = [qt, pl.BlockSpec((tq, LANES), lambda b, h, i: (b * tps + i, h)),
             pl.BlockSpec((1, tk, 2 * hw), lambda b, h, i: (b, 0, h)),
             pl.BlockSpec((1, tk, LANES), lambda b, h, i: (b, 0, 0)), qt]
    args = [q_nope, q_pe, kv, kpe, gate]
    if rope:
        tab = pl.BlockSpec((tq, LANES), lambda b, h, i: (i, 0))
        specs += [tab, tab]
        args += list(_rope_tables(seq, C_ROPE))
    return pl.pallas_call(
        functools.partial(_mla_attn_kernel, rope=rope),
        grid=(batch, pairs, tps),
        in_specs=specs,
        out_specs=qt,
        out_shape=jax.ShapeDtypeStruct((n, w), BF16),
        compiler_params=_cparams("parallel", "parallel", "parallel"),
        name="mla_attention",
    )(*args)


def _trunk(x3, cond, mods, wts, caches):
    bsz, seq, d = x3.shape
    n = bsz * seq
    x = x3.reshape(n, d)
    rope = caches is not None
    ab, mla = wts["ab"], wts["mla"]

    def mod_rows(layer):
        m = mods[layer]
        return [m[:, k * d:(k + 1) * d].reshape(-1, 1, d) for k in range(3)]

    shift, scale, gate = mod_rows(0)
    h = rms_norm(x, wts["norm_g"][0], BF16, scale, shift, cond)
    mm_in = lambda w, dt: matmul([h], [w], dt, 1024, 512)
    shifted = mm_in(ab["w_shift"], F32)
    g_a = mm_in(ab["w_ga"], BF16)
    q_b = mm_in(ab["w_q"], F32)
    k_b = mm_in(ab["w_k"], F32)
    v_b = mm_in(ab["w_v"], F32)
    g_b = mm_in(ab["w_gb"], BF16)

    r, v, kk, b, kd, lw, bonus = rwkv_prep(shifted, ab, seq=seq)
    width = r.shape[1]
    if caches is None:
        s0 = jnp.zeros((bsz, 2, width // LANES, LANES, LANES), F32)
    else:
        s0 = _state_to_block_diag(caches[0])
    y, s_fin = rwkv_scan(r, v, kk, b, kd, lw, s0, batch=bsz, seq=seq)
    mixed_a = rwkv_post(y, bonus, g_a, ab["ln_w"], ab["ln_b"])

    q_s, k_f, k_bf, v_bf = gqa_prep(q_b, k_b, v_b, ab["q_g"], ab["k_g"], seq=seq, rope=rope)
    kw = k_bf.shape[1]
    keys = k_bf.reshape(bsz, seq, kw)
    vals = v_bf.reshape(bsz, seq, kw)
    if caches is not None:
        keys = jnp.concatenate([caches[1].reshape(bsz, -1, kw).astype(BF16), keys], axis=1)
        vals = jnp.concatenate([caches[2].reshape(bsz, -1, kw).astype(BF16), vals], axis=1)
    mixed_b = gqa_attention(q_s, keys, vals, g_b, batch=bsz, seq=seq, tq=128 if rope else 256)
    x = matmul([mixed_a, mixed_b], [ab["w_out_a"], ab["w_out_b"]], F32, 1024, 512,
               resid=x, gate=gate, cond=cond)

    shift, scale, gate = mod_rows(1)
    h = rms_norm(x, wts["norm_g"][1], BF16, scale, shift, cond)
    qkv_a = matmul([h], [mla["w_qkv"]], F32, 1024, 512)
    kpe = matmul([h], [mla["w_kpe"]], F32, 1024, LANES)
    g_c = matmul([h], [mla["w_gate"]], BF16, 1024, 512)
    qa_n, ckv_f, ckv_b, kpe_b = mla_prep(qkv_a, kpe, mla["q_a_g"], mla["kv_a_g"], seq=seq, rope=rope)
    q_scale = (C_NOPE + C_ROPE) ** -0.5
    q_nope = matmul([qa_n], [mla["w_uq_nope"]], BF16, 1024, 1024, out_scale=q_scale)
    q_pe = matmul([qa_n], [mla["w_uq_pe"]], F32, 1024, 1024, out_scale=q_scale)
    ckv_all = ckv_b.reshape(bsz, seq, C_KV_LORA)
    kpe_all = kpe_b.reshape(bsz, seq, LANES)
    if caches is not None:
        ckv_all = jnp.concatenate([caches[3].astype(BF16), ckv_all], axis=1)
        ctx_kpe = caches[4].astype(BF16)
        kpe_all = jnp.concatenate([jnp.concatenate([ctx_kpe, ctx_kpe], axis=-1), kpe_all], axis=1)
    tk = ckv_all.shape[1]
    kv = matmul([ckv_all.reshape(bsz * tk, C_KV_LORA)], [mla["w_ukv"]], BF16, 1024, 2048)
    o = mla_attention(q_nope, q_pe, kv.reshape(bsz, tk, -1), kpe_all, g_c,
                      batch=bsz, seq=seq, tq=512 if rope else 256, rope=rope)
    x = matmul([o], [mla["w_out"]], F32, 1024, 256, resid=x, gate=gate, cond=cond)

    y_out = rms_norm(x, wts["final_g"], F32).reshape(bsz, seq, d)
    if caches is not None:
        return y_out, None
    new_state = _state_from_block_diag(s_fin)[:, None]
    new_k = k_f.reshape(bsz, 1, seq, B_KV_HEADS, HEAD_B)
    new_v = v_b.reshape(bsz, 1, seq, B_KV_HEADS, HEAD_B)
    new_ckv = ckv_f.reshape(bsz, 1, seq, C_KV_LORA)
    new_kpe = kpe[:, :C_ROPE].reshape(bsz, 1, seq, C_ROPE)
    return y_out, (new_state, new_k, new_v, new_ckv, new_kpe)


def kernel(x_prompt, x_sample, c, state_rwkv, cache_gqa_k, cache_gqa_v, cache_mla_ckv, cache_mla_kpe, c_ctx, mod_w, mod_b, norm_g, ab_w_in, ab_mix, rwkv_w0, rwkv_w2, rwkv_a0, rwkv_a2, rwkv_k_k, rwkv_k_a, rwkv_r_k, rwkv_ln_w, rwkv_ln_b, gqa_q_g, gqa_k_g, ab_w_out, mla_w_in, mla_q_a_g, mla_kv_a_g, mla_w_uq, mla_w_ukv, mla_w_out, final_g):
    d = x_prompt.shape[-1]
    dec_batch, dec_seq, _ = x_sample.shape
    a_width = rwkv_k_k.shape[-1]
    a_shift = ab_mix.shape[-1]
    b_width = a_width
    b_kv = B_KV_HEADS * HEAD_B
    c_heads = mla_w_out.shape[1] // C_V

    cond = jnp.concatenate([c_ctx[None], c, jnp.zeros((SUBLANES - 1 - dec_batch, d), F32)], axis=0)
    mods = modulation(cond, mod_w, mod_b)

    w_in = ab_w_in[0]
    edges = np.cumsum([0, a_shift, a_width, b_width, b_kv, b_kv, b_width])
    seg = lambda i: w_in[:, edges[i]:edges[i + 1]].astype(BF16)
    ab = {"w_shift": seg(0), "w_ga": seg(1), "w_q": seg(2), "w_k": seg(3), "w_v": seg(4), "w_gb": seg(5),
          "mix": ab_mix[0], "w0": rwkv_w0[0], "w2": rwkv_w2[0], "a0": rwkv_a0[0], "a2": rwkv_a2[0],
          "k_k": rwkv_k_k[0], "k_a": rwkv_k_a[0], "r_k": rwkv_r_k[0], "ln_w": rwkv_ln_w[0],
          "ln_b": rwkv_ln_b[0], "q_g": gqa_q_g[0], "k_g": gqa_k_g[0],
          "w_out_a": ab_w_out[0, :a_width].astype(BF16), "w_out_b": ab_w_out[0, a_width:].astype(BF16)}

    m_in = mla_w_in[0]
    n_qkv = C_Q_LORA + C_KV_LORA
    w_kpe = jnp.pad(m_in[:, n_qkv:n_qkv + C_ROPE], ((0, 0), (0, LANES - C_ROPE))).astype(BF16)
    w_uq = mla_w_uq[0].reshape(C_Q_LORA, c_heads, C_NOPE + C_ROPE)
    mla = {"w_qkv": m_in[:, :n_qkv].astype(BF16), "w_kpe": w_kpe,
           "w_gate": m_in[:, n_qkv + C_ROPE:].astype(BF16),
           "q_a_g": mla_q_a_g[0], "kv_a_g": mla_kv_a_g[0],
           "w_uq_nope": w_uq[:, :, :C_NOPE].reshape(C_Q_LORA, -1).astype(BF16),
           "w_uq_pe": w_uq[:, :, C_NOPE:].reshape(C_Q_LORA, -1).astype(BF16),
           "w_ukv": mla_w_ukv[0].astype(BF16), "w_out": mla_w_out[0].astype(BF16)}
    wts = {"ab": ab, "mla": mla, "norm_g": norm_g, "final_g": final_g}

    y_prompt, ctx_out = _trunk(x_prompt, Cond(0, 0, x_prompt.shape[1]), mods, wts, None)
    caches = (state_rwkv[:, 0], cache_gqa_k[:, 0], cache_gqa_v[:, 0], cache_mla_ckv[:, 0], cache_mla_kpe[:, 0])
    y_sample, _ = _trunk(x_sample, Cond(1, 1, dec_seq), mods, wts, caches)
    return (y_prompt, y_sample) + ctx_out
```

```python
import functools
from typing import NamedTuple

import jax
import jax.numpy as jnp
import numpy as np
from jax import lax
from jax.experimental import pallas as pl
from jax.experimental.pallas import tpu as pltpu

F32 = jnp.float32
BF16 = jnp.bfloat16

LANES = 128
SUBLANES = 8
MXU_DIM = 256
VMEM_LIMIT_BYTES = 56 * 1024 * 1024

NORM_EPS = 1e-6
GN_EPS = 64e-5
ROPE_THETA = 10000.0
GRID_W = 64
HEAD_A = 64
HEAD_B = 128
B_KV_HEADS = 4
C_NOPE = 128
C_ROPE = 64
C_V = 128
C_Q_LORA = 1536
C_KV_LORA = 512
LORA = 128
RW_CHUNK = 64

NN = (((1,), (0,)), ((), ()))
NT = (((1,), (1,)), ((), ()))


class Cond(NamedTuple):
    base: int
    per_batch: int
    seq: int

    def row(self, i, tile_rows):
        return self.base + (i * tile_rows // self.seq) * self.per_batch


def _cparams(*sem):
    return pltpu.CompilerParams(dimension_semantics=sem, vmem_limit_bytes=VMEM_LIMIT_BYTES)


def _sigmoid(x):
    return 1.0 / (1.0 + jnp.exp(-x))


def _split(x):
    hi = x.astype(BF16)
    lo = (x - hi.astype(F32)).astype(BF16)
    return hi, lo


def _dot(a, b, dims=NN, passes=1):
    if passes == 1:
        return lax.dot_general(a.astype(BF16), b.astype(BF16), dims, preferred_element_type=F32)
    ah, al = _split(a)
    bh, bl = _split(b)
    out = lax.dot_general(ah, bh, dims, preferred_element_type=F32)
    out = out + lax.dot_general(ah, bl, dims, preferred_element_type=F32)
    return out + lax.dot_general(al, bh, dims, preferred_element_type=F32)


def _group_sum(x, ones_bd):
    hi, lo = _split(x)
    outs = []
    for g in range(x.shape[1] // MXU_DIM):
        sl = slice(g * MXU_DIM, (g + 1) * MXU_DIM)
        outs.append(jnp.dot(hi[:, sl], ones_bd, preferred_element_type=F32)
                    + jnp.dot(lo[:, sl], ones_bd, preferred_element_type=F32))
    return outs[0] if len(outs) == 1 else jnp.concatenate(outs, axis=-1)


def _ones_block_diag(block):
    idx = np.arange(MXU_DIM) // block
    return jnp.asarray(idx[:, None] == idx[None, :], dtype=BF16)


def _rope(x, cos_e, sin_e):
    even = lax.broadcasted_iota(jnp.int32, cos_e.shape, 1) % 2 == 0
    outs = []
    for h in range(x.shape[1] // LANES):
        xh = x[:, h * LANES:(h + 1) * LANES]
        partner = jnp.where(even, pltpu.roll(xh, LANES - 1, 1), pltpu.roll(xh, 1, 1))
        outs.append(xh * cos_e + partner * sin_e)
    return outs[0] if len(outs) == 1 else jnp.concatenate(outs, axis=-1)


def _rope_tables(n_tokens, rot_dim):
    rows = n_tokens // GRID_W
    row = jnp.repeat(jnp.arange(rows, dtype=F32), GRID_W)
    col = jnp.tile(jnp.arange(GRID_W, dtype=F32), rows)
    n_freq = rot_dim // 4
    inv_freq = ROPE_THETA ** (-jnp.arange(n_freq, dtype=F32) / n_freq)
    ang = jnp.concatenate([row[:, None] * inv_freq, col[:, None] * inv_freq], axis=-1)
    cos = jnp.repeat(jnp.cos(ang), 2, axis=-1)
    sin = jnp.repeat(jnp.sin(ang), 2, axis=-1) * jnp.tile(jnp.asarray([-1.0, 1.0], F32), rot_dim // 2)
    reps = LANES // rot_dim
    return jnp.tile(cos, (1, reps)), jnp.tile(sin, (1, reps))


def _mod_kernel(c_ref, w_ref, b_ref, o_ref):
    c = c_ref[...]
    s = c * _sigmoid(c)
    o_ref[0] = jnp.dot(s.astype(BF16), w_ref[0].astype(BF16), preferred_element_type=F32) + b_ref[0]


def modulation(cond, mod_w, mod_b, tn=512):
    depth, d, n = mod_w.shape
    rows = cond.shape[0]
    return pl.pallas_call(
        _mod_kernel,
        grid=(depth, n // tn),
        in_specs=[pl.BlockSpec((rows, d), lambda l, j: (0, 0)),
                  pl.BlockSpec((1, d, tn), lambda l, j: (l, 0, j)),
                  pl.BlockSpec((1, 1, tn), lambda l, j: (l, 0, j))],
        out_specs=pl.BlockSpec((1, rows, tn), lambda l, j: (l, 0, j)),
        out_shape=jax.ShapeDtypeStruct((depth, rows, n), F32),
        compiler_params=_cparams("parallel", "parallel"),
        name="modulation",
    )(cond, mod_w, mod_b.reshape(depth, 1, n))


def _norm_kernel(x_ref, g_ref, *rest, mod):
    x = x_ref[...]
    y = x * lax.rsqrt(jnp.mean(x * x, axis=-1, keepdims=True) + NORM_EPS) * g_ref[...]
    if mod:
        sc_ref, sh_ref, o_ref = rest
        y = y * (1.0 + sc_ref[0]) + sh_ref[0]
    else:
        (o_ref,) = rest
    o_ref[...] = y.astype(o_ref.dtype)


def rms_norm(x, g, out_dtype, scale=None, shift=None, cond=None, tr=256):
    n, d = x.shape
    row = pl.BlockSpec((tr, d), lambda i: (i, 0))
    specs = [row, pl.BlockSpec((1, d), lambda i: (0, 0))]
    args = [x, g.reshape(1, d)]
    if cond is not None:
        cspec = pl.BlockSpec((1, 1, d), lambda i: (cond.row(i, tr), 0, 0))
        specs += [cspec, cspec]
        args += [scale, shift]
    return pl.pallas_call(
        functools.partial(_norm_kernel, mod=cond is not None),
        grid=(n // tr,),
        in_specs=specs,
        out_specs=row,
        out_shape=jax.ShapeDtypeStruct((n, d), out_dtype),
        compiler_params=_cparams("parallel"),
        name="rms_norm",
    )(*args)


def _mm_kernel(*refs, n_pairs, resid, out_scale):
    acc = None
    for p in range(n_pairs):
        part = jnp.dot(refs[p][...], refs[n_pairs + p][...], preferred_element_type=F32)
        acc = part if acc is None else acc + part
    if out_scale != 1.0:
        acc = acc * out_scale
    if resid:
        x_ref, gate_ref, o_ref = refs[2 * n_pairs:]
        acc = x_ref[...] + gate_ref[0] * acc
    else:
        (o_ref,) = refs[2 * n_pairs:]
    o_ref[...] = acc.astype(o_ref.dtype)


def matmul(a_list, w_list, out_dtype, tm, tn, resid=None, gate=None, cond=None, out_scale=1.0):
    m = a_list[0].shape[0]
    n = w_list[0].shape[1]
    specs = [pl.BlockSpec((tm, a.shape[1]), lambda i, j: (i, 0)) for a in a_list]
    specs += [pl.BlockSpec((w.shape[0], tn), lambda i, j: (0, j)) for w in w_list]
    args = list(a_list) + list(w_list)
    tile = pl.BlockSpec((tm, tn), lambda i, j: (i, j))
    if resid is not None:
        specs += [tile, pl.BlockSpec((1, 1, tn), lambda i, j: (cond.row(i, tm), 0, j))]
        args += [resid, gate]
    return pl.pallas_call(
        functools.partial(_mm_kernel, n_pairs=len(a_list), resid=resid is not None, out_scale=out_scale),
        grid=(m // tm, n // tn),
        in_specs=specs,
        out_specs=tile,
        out_shape=jax.ShapeDtypeStruct((m, n), out_dtype),
        compiler_params=_cparams("parallel", "parallel"),
        name="matmul",
    )(*args)


def _rwkv_prep_kernel(x_ref, xp_ref, xn_ref, mix_ref, w0_ref, w2_ref, a0_ref, a2_ref, kk_ref, ka_ref,
                      rk_ref, ones_ref, r_out, v_out, kk_out, b_out, kd_out, lw_out, bonus_out, *,
                      tiles_per_seq, width):
    i = pl.program_id(0)
    x = x_ref[...]
    tr = x.shape[0]
    pos = i % tiles_per_seq
    prev_row = jnp.where(pos == 0, 0.0, xp_ref[SUBLANES - 1:SUBLANES, :])
    next_row = jnp.where(pos == tiles_per_seq - 1, 0.0, xn_ref[0:1, :])
    rid = lax.broadcasted_iota(jnp.int32, (tr, 1), 0)
    prev = jnp.where(rid == 0, prev_row, pltpu.roll(x, 1, 0))
    nxt = jnp.where(rid == tr - 1, next_row, pltpu.roll(x, tr - 1, 0))
    s = x + mix_ref[...] * (0.5 * (prev + nxt) - x)
    r = s[:, :width]
    k = s[:, width:2 * width]
    v = s[:, 2 * width:3 * width]
    ones_bd = ones_ref[...]
    kkr = k * kk_ref[...]
    kk = kkr / jnp.maximum(jnp.sqrt(_group_sum(kkr * kkr, ones_bd)), 1e-12)
    r_out[...] = r
    v_out[...] = v
    kk_out[...] = kk
    rk = r * rk_ref[...]
    bonus = None
    for d in range(2):
        base = 3 * width + 2 * LORA * d
        wl = s[:, base:base + LORA]
        al = s[:, base + LORA:base + 2 * LORA]
        w_pre = w0_ref[d] + jnp.dot(jnp.tanh(wl).astype(BF16), w2_ref[d], preferred_element_type=F32)
        lw_out[d] = -jnp.exp(-0.5) * _sigmoid(w_pre)
        a = _sigmoid(a0_ref[d] + jnp.dot(al.astype(BF16), a2_ref[d], preferred_element_type=F32))
        kd = k * (1.0 + (a - 1.0) * ka_ref[...])
        kd_out[d] = kd
        b_out[d] = kk * a
        bd = _group_sum(rk * kd, ones_bd)
        bonus = bd if bonus is None else bonus + bd
    bonus_out[...] = bonus * v


def rwkv_prep(shifted, p, *, seq, tr=128):
    n, cols = shifted.shape
    width = p["k_k"].shape[-1]
    nb = n // SUBLANES
    per_tile = tr // SUBLANES
    row = lambda c: pl.BlockSpec((1, c), lambda i: (0, 0))
    d3 = lambda a, b: pl.BlockSpec((2, a, b), lambda i: (0, 0, 0))
    shared = pl.BlockSpec((tr, width), lambda i: (i, 0))
    perdir = pl.BlockSpec((2, tr, width), lambda i: (0, i, 0))
    f32 = lambda shape: jax.ShapeDtypeStruct(shape, F32)
    return pl.pallas_call(
        functools.partial(_rwkv_prep_kernel, tiles_per_seq=seq // tr, width=width),
        grid=(n // tr,),
        in_specs=[pl.BlockSpec((tr, cols), lambda i: (i, 0)),
                  pl.BlockSpec((SUBLANES, cols), lambda i: (jnp.maximum(i * per_tile - 1, 0), 0)),
                  pl.BlockSpec((SUBLANES, cols), lambda i: (jnp.minimum((i + 1) * per_tile, nb - 1), 0)),
                  row(cols), d3(1, width), d3(LORA, width), d3(1, width), d3(LORA, width),
                  row(width), row(width), row(width),
                  pl.BlockSpec((MXU_DIM, MXU_DIM), lambda i: (0, 0))],
        out_specs=[shared, shared, shared, perdir, perdir, perdir, shared],
        out_shape=[f32((n, width))] * 3 + [f32((2, n, width))] * 3 + [f32((n, width))],
        compiler_params=_cparams("parallel"),
        name="rwkv_prep",
    )(shifted, shifted, shifted, p["mix"].reshape(1, cols), p["w0"].reshape(2, 1, width),
      p["w2"].astype(BF16), p["a0"].reshape(2, 1, width), p["a2"].astype(BF16),
      p["k_k"].reshape(1, width), p["k_a"].reshape(1, width), p["r_k"].reshape(1, width),
      _ones_block_diag(HEAD_A))


def _each(fn, *lists):
    return [fn(*xs) for xs in zip(*lists)]


def _rwkv_chunks_local(units, tri, strict, incl, eye, p0, xr):
    C = RW_CHUNK
    c2 = 2 * C
    r, v, kk, b, kd, lw = (list(x) for x in zip(*units))

    def cumsum(x):
        hi, lo = _split(x)
        lo2 = (x - hi.astype(F32) - lo.astype(F32)).astype(BF16)
        return (jnp.dot(tri, hi, preferred_element_type=F32) + jnp.dot(tri, lo, preferred_element_type=F32)
                + jnp.dot(tri, lo2, preferred_element_type=F32))

    def stack(x):
        return jnp.concatenate([jnp.where(p0, x, 0.0), jnp.where(p0, 0.0, x)], axis=0)

    L = _each(cumsum, lw)
    tot = _each(lambda x: jnp.sum(x, axis=0, keepdims=True), lw)
    gi = _each(lambda l: jnp.exp(-l), L)
    er = _each(lambda t_, l: jnp.exp(t_ - l), tot, L)
    AV = _each(lambda k_, l, w_: stack(-k_ * jnp.exp(l - w_)), kk, L, lw)
    RT = _each(lambda r_, l: stack(r_ * jnp.exp(l)), r, L)
    VV = _each(stack, v)
    bt = _each(jnp.multiply, b, gi)
    kt = _each(jnp.multiply, kd, gi)
    o1 = _each(lambda a_, r_, b_, k_: _dot(jnp.concatenate([a_, r_], axis=0),
                                           jnp.concatenate([b_, b_, k_, k_], axis=0), NT), AV, RT, bt, kt)
    a_ab = _each(lambda o: jnp.where(strict, o[:c2, :c2], 0.0), o1)
    a_ak = _each(lambda o: jnp.where(strict, o[:c2, c2:], 0.0), o1)
    r_bk = _each(lambda o: jnp.where(jnp.concatenate([incl, incl], axis=1), o[c2:], 0.0), o1)
    z1 = _each(_dot, a_ak, VV)
    T = _each(lambda a_: eye + jnp.where(xr < 2, a_, 0.0), a_ab)
    half = 2
    while half < C:
        E = _each(lambda a_: jnp.where(xr >= half, jnp.where(xr < 2 * half, a_, 0.0), 0.0), a_ab)
        ET = _each(_dot, E, T)
        T = _each(lambda t_, et_: t_ + _dot(t_, et_), T, ET)
        half *= 2
    au = _each(lambda t_, a_, z_: _dot(t_, jnp.concatenate([a_, z_], axis=1)), T, AV, z1)
    Z = _each(lambda a_, v_: jnp.concatenate([a_, jnp.concatenate([jnp.zeros_like(v_), v_], axis=1)], axis=0),
              au, VV)
    ry = _each(_dot, r_bk, Z)
    rp = _each(lambda r_, y_: r_ + y_[:, :c2], RT, ry)
    y0 = _each(lambda y_: y_[:, c2:], ry)
    W = _each(lambda b_, k_, e_: jnp.concatenate([stack(b_ * e_), stack(k_ * e_)], axis=0).T, b, kd, er)
    mn = _each(lambda w_, z_: _dot(w_, z_, NN, 3), W, Z)
    M = _each(lambda t_, m_: jnp.where(eye > 0.5, jnp.exp(t_), 0.0) + m_[:, :c2], tot, mn)
    n0 = _each(lambda m_: m_[:, c2:], mn)
    return rp, y0, M, n0


def _rwkv_kernel(r_ref, v_ref, kk_ref, b_ref, kd_ref, lw_ref, s0_ref, y_ref, st_ref, h_scr, *, G, TB):
    C = RW_CHUNK
    d = pl.program_id(0)
    t = pl.program_id(3)
    nt = pl.num_programs(3)
    nc = TB // C
    rev = d == 1

    @pl.when(t == 0)
    def _():
        h_scr[...] = s0_ref[0, 0]

    c2 = 2 * C
    row = lax.broadcasted_iota(jnp.int32, (c2, c2), 0)
    col = lax.broadcasted_iota(jnp.int32, (c2, c2), 1)
    same = (row >= C) == (col >= C)
    diff = row - col
    diff = jnp.where(rev, -diff, diff)
    diff = jnp.where(same, diff, -1)
    strict = diff > 0
    incl = diff >= 0
    eye = (row == col).astype(F32)
    tr = lax.broadcasted_iota(jnp.int32, (C, C), 0) - lax.broadcasted_iota(jnp.int32, (C, C), 1)
    tr = jnp.where(rev, -tr, tr)
    tri = (tr >= 0).astype(BF16)
    p0 = lax.broadcasted_iota(jnp.int32, (C, LANES), 1) < HEAD_A

    rows = [pl.ds(pl.multiple_of(jnp.where(rev, nc - 1 - ci, ci) * C, C), C) for ci in range(nc)]
    lanes = [slice(p * LANES, (p + 1) * LANES) for p in range(G)]
    units = [(r_ref[rs, ls], v_ref[rs, ls], kk_ref[rs, ls], b_ref[0, rs, ls], kd_ref[0, rs, ls],
              lw_ref[0, rs, ls]) for rs in rows for ls in lanes]
    rp, y0, M, n0 = _rwkv_chunks_local(units, tri, strict, incl, eye, p0, row ^ col)
    H = [h_scr[p] for p in range(G)]
    for ci, rs in enumerate(rows):
        sl = slice(ci * G, (ci + 1) * G)
        Y = _each(lambda a_, h_, c_: _dot(a_, h_) + c_, rp[sl], H, y0[sl])
        H = _each(lambda m_, h_, c_: _dot(m_, h_) + c_, M[sl], H, n0[sl])
        for ls, y in zip(lanes, Y):
            y_ref[0, rs, ls] = y[:C] + y[C:]
    for p in range(G):
        h_scr[p] = H[p]

    @pl.when(t == nt - 1)
    def _():
        st_ref[0, 0] = h_scr[...]


def rwkv_scan(r, v, kk, b, kd, lw, s0, *, batch, seq, G=4, TB=256):
    n, width = r.shape
    assert n == batch * seq and seq % TB == 0 and TB % RW_CHUNK == 0
    pairs = width // LANES
    assert pairs % G == 0
    nt = seq // TB

    def tmap(dd, bb, tt):
        return bb * nt + tt + dd * (nt - 1 - 2 * tt)

    shared = pl.BlockSpec((TB, G * LANES), lambda dd, bb, gg, tt: (tmap(dd, bb, tt), gg))
    perdir = pl.BlockSpec((1, TB, G * LANES), lambda dd, bb, gg, tt: (dd, tmap(dd, bb, tt), gg))
    state = pl.BlockSpec((1, 1, G, LANES, LANES), lambda dd, bb, gg, tt: (bb, dd, gg, 0, 0))
    return pl.pallas_call(
        functools.partial(_rwkv_kernel, G=G, TB=TB),
        grid=(2, batch, pairs // G, nt),
        in_specs=[shared, shared, shared, perdir, perdir, perdir, state],
        out_specs=[perdir, state],
        out_shape=[jax.ShapeDtypeStruct((2, n, width), F32),
                   jax.ShapeDtypeStruct(s0.shape, F32)],
        scratch_shapes=[pltpu.VMEM((G, LANES, LANES), F32)],
        compiler_params=_cparams("parallel", "parallel", "parallel", "arbitrary"),
        name="rwkv_scan",
    )(r, v, kk, b, kd, lw, s0)


def _state_to_block_diag(s):
    bsz, _, heads, n, _ = s.shape
    st = jnp.swapaxes(s, -1, -2).reshape(bsz, 2, heads // 2, 2, n, n)
    z = jnp.zeros_like(st[:, :, :, 0])
    top = jnp.concatenate([st[:, :, :, 0], z], axis=-1)
    bot = jnp.concatenate([z, st[:, :, :, 1]], axis=-1)
    return jnp.concatenate([top, bot], axis=-2)


def _state_from_block_diag(sbd):
    bsz, _, pairs, _, _ = sbd.shape
    n = HEAD_A
    blocks = jnp.stack([sbd[:, :, :, :n, :n], sbd[:, :, :, n:, n:]], axis=3)
    return jnp.swapaxes(blocks.reshape(bsz, 2, 2 * pairs, n, n), -1, -2)


def _rwkv_post_kernel(y_ref, bonus_ref, gate_ref, lnw_ref, lnb_ref, ones_ref, o_ref):
    ones_bd = ones_ref[...]
    y = y_ref[0] + y_ref[1]
    mean = _group_sum(y, ones_bd) * (1.0 / HEAD_A)
    yc = y - mean
    var = _group_sum(yc * yc, ones_bd) * (1.0 / HEAD_A)
    out = yc * lax.rsqrt(var + GN_EPS) * lnw_ref[...] + lnb_ref[...] + bonus_ref[...]
    gt = gate_ref[...].astype(F32)
    o_ref[...] = (out * gt * _sigmoid(gt)).astype(o_ref.dtype)


def rwkv_post(y, bonus, gate, ln_w, ln_b, tr=256):
    _, n, width = y.shape
    tile = pl.BlockSpec((tr, width), lambda i: (i, 0))
    row = pl.BlockSpec((1, width), lambda i: (0, 0))
    return pl.pallas_call(
        _rwkv_post_kernel,
        grid=(n // tr,),
        in_specs=[pl.BlockSpec((2, tr, width), lambda i: (0, i, 0)), tile, tile, row, row,
                  pl.BlockSpec((MXU_DIM, MXU_DIM), lambda i: (0, 0))],
        out_specs=tile,
        out_shape=jax.ShapeDtypeStruct((n, width), BF16),
        compiler_params=_cparams("parallel"),
        name="rwkv_post",
    )(y, bonus, gate, ln_w.reshape(1, width), ln_b.reshape(1, width), _ones_block_diag(HEAD_A))


def _gqa_prep_kernel(q_ref, k_ref, v_ref, qg_ref, kg_ref, ones_ref, *rest, rope, scale):
    if rope:
        cos_ref, sin_ref, qo_ref, kf_ref, kb_ref, vb_ref = rest
    else:
        qo_ref, kf_ref, kb_ref, vb_ref = rest
    ones_bd = ones_ref[...]

    def norm(x, g):
        ms = _group_sum(x * x, ones_bd) * (1.0 / HEAD_B)
        return x * lax.rsqrt(ms + NORM_EPS) * g

    q = norm(q_ref[...], qg_ref[...])
    k = norm(k_ref[...], kg_ref[...])
    if rope:
        q = _rope(q, cos_ref[...], sin_ref[...])
        k = _rope(k, cos_ref[...], sin_ref[...])
    qo_ref[...] = (q * scale).astype(BF16)
    kf_ref[...] = k
    kb_ref[...] = k.astype(BF16)
    vb_ref[...] = v_ref[...].astype(BF16)


def gqa_prep(q, k, v, q_g, k_g, *, seq, rope, tr=256):
    n, qw = q.shape
    kw = k.shape[1]
    tps = seq // tr
    qt = pl.BlockSpec((tr, qw), lambda i: (i, 0))
    kt = pl.BlockSpec((tr, kw), lambda i: (i, 0))
    specs = [qt, kt, kt, pl.BlockSpec((1, qw), lambda i: (0, 0)), pl.BlockSpec((1, kw), lambda i: (0, 0)),
             pl.BlockSpec((MXU_DIM, MXU_DIM), lambda i: (0, 0))]
    args = [q, k, v, jnp.tile(q_g, qw // HEAD_B).reshape(1, qw), jnp.tile(k_g, kw // HEAD_B).reshape(1, kw),
            _ones_block_diag(HEAD_B)]
    if rope:
        tab = pl.BlockSpec((tr, LANES), lambda i: (i % tps, 0))
        specs += [tab, tab]
        args += list(_rope_tables(seq, HEAD_B))
    return pl.pallas_call(
        functools.partial(_gqa_prep_kernel, rope=rope, scale=HEAD_B ** -0.5),
        grid=(n // tr,),
        in_specs=specs,
        out_specs=[qt, kt, kt, kt],
        out_shape=[jax.ShapeDtypeStruct((n, qw), BF16), jax.ShapeDtypeStruct((n, kw), F32),
                   jax.ShapeDtypeStruct((n, kw), BF16), jax.ShapeDtypeStruct((n, kw), BF16)],
        compiler_params=_cparams("parallel"),
        name="gqa_prep",
    )(*args)


def _attend(q, k, v):
    s = lax.dot_general(k, q, NT, preferred_element_type=F32)
    m = jnp.max(s, axis=0, keepdims=True)
    p = jnp.exp(s - m)
    l = jnp.sum(p, axis=0, keepdims=True)
    o = jnp.dot(v.T, p.astype(BF16), preferred_element_type=F32)
    return (o / l).T


def _gqa_attn_kernel(q_ref, k_ref, v_ref, g_ref, o_ref, *, group):
    q = q_ref[...]
    tq = q.shape[0]
    q4 = jnp.concatenate([q[:, g * HEAD_B:(g + 1) * HEAD_B] for g in range(group)], axis=0)
    o = _attend(q4, k_ref[0], v_ref[0])
    o = jnp.concatenate([o[g * tq:(g + 1) * tq] for g in range(group)], axis=-1)
    gt = g_ref[...].astype(F32)
    o_ref[...] = (o * gt * _sigmoid(gt)).astype(o_ref.dtype)


def gqa_attention(q, keys, vals, gate, *, batch, seq, tq):
    n, qw = q.shape
    tk = keys.shape[1]
    group = qw // HEAD_B // B_KV_HEADS
    gw = group * HEAD_B
    tps = seq // tq
    qt = pl.BlockSpec((tq, gw), lambda b, h, i: (b * tps + i, h))
    kvt = pl.BlockSpec((1, tk, HEAD_B), lambda b, h, i: (b, 0, h))
    return pl.pallas_call(
        functools.partial(_gqa_attn_kernel, group=group),
        grid=(batch, B_KV_HEADS, tps),
        in_specs=[qt, kvt, kvt, qt],
        out_specs=qt,
        out_shape=jax.ShapeDtypeStruct((n, qw), BF16),
        compiler_params=_cparams("parallel", "parallel", "parallel"),
        name="gqa_attention",
    )(q, keys, vals, gate)


def _mla_prep_kernel(qkv_ref, kpe_ref, qg_ref, kvg_ref, *rest, rope):
    if rope:
        cos_ref, sin_ref, qa_ref, ckv_f_ref, ckv_b_ref, kpe_b_ref = rest
    else:
        qa_ref, ckv_f_ref, ckv_b_ref, kpe_b_ref = rest

    def norm(x, g):
        return x * lax.rsqrt(jnp.mean(x * x, axis=-1, keepdims=True) + NORM_EPS) * g

    x = qkv_ref[...]
    qa_ref[...] = norm(x[:, :C_Q_LORA], qg_ref[...]).astype(BF16)
    ckv = norm(x[:, C_Q_LORA:], kvg_ref[...])
    ckv_f_ref[...] = ckv
    ckv_b_ref[...] = ckv.astype(BF16)
    kpe = kpe_ref[...]
    if rope:
        kpe = _rope(kpe, cos_ref[...], sin_ref[...])
    kpe_b_ref[...] = (kpe + pltpu.roll(kpe, C_ROPE, 1)).astype(BF16)


def mla_prep(qkv_a, kpe, q_g, kv_g, *, seq, rope, tr=256):
    n, w = qkv_a.shape
    tps = seq // tr
    tile = lambda c: pl.BlockSpec((tr, c), lambda i: (i, 0))
    row = lambda c: pl.BlockSpec((1, c), lambda i: (0, 0))
    specs = [tile(w), tile(LANES), row(C_Q_LORA), row(C_KV_LORA)]
    args = [qkv_a, kpe, q_g.reshape(1, C_Q_LORA), kv_g.reshape(1, C_KV_LORA)]
    if rope:
        tab = pl.BlockSpec((tr, LANES), lambda i: (i % tps, 0))
        specs += [tab, tab]
        args += list(_rope_tables(seq, C_ROPE))
    return pl.pallas_call(
        functools.partial(_mla_prep_kernel, rope=rope),
        grid=(n // tr,),
        in_specs=specs,
        out_specs=[tile(C_Q_LORA), tile(C_KV_LORA), tile(C_KV_LORA), tile(LANES)],
        out_shape=[jax.ShapeDtypeStruct((n, C_Q_LORA), BF16), jax.ShapeDtypeStruct((n, C_KV_LORA), F32),
                   jax.ShapeDtypeStruct((n, C_KV_LORA), BF16), jax.ShapeDtypeStruct((n, LANES), BF16)],
        compiler_params=_cparams("parallel"),
        name="mla_prep",
    )(*args)


def _mla_attn_kernel(qn_ref, qp_ref, kv_ref, kp_ref, g_ref, *rest, rope):
    if rope:
        cos_ref, sin_ref, o_ref = rest
    else:
        (o_ref,) = rest
    qp = qp_ref[...]
    if rope:
        qp = _rope(qp, cos_ref[...], sin_ref[...])
    first = lax.broadcasted_iota(jnp.int32, qp.shape, 1) < C_ROPE
    kp = kp_ref[0]
    hw = C_NOPE + C_V
    outs = []
    for j in range(2):
        qpj = jnp.where(first if j == 0 else jnp.logical_not(first), qp, 0.0).astype(BF16)
        qj = jnp.concatenate([qn_ref[:, j * C_NOPE:(j + 1) * C_NOPE], qpj], axis=-1)
        kj = jnp.concatenate([kv_ref[0, :, j * hw:j * hw + C_NOPE], kp], axis=-1)
        outs.append(_attend(qj, kj, kv_ref[0, :, j * hw + C_NOPE:(j + 1) * hw]))
    o = jnp.concatenate(outs, axis=-1)
    gt = g_ref[...].astype(F32)
    o_ref[...] = (o * gt * _sigmoid(gt)).astype(o_ref.dtype)


def mla_attention(q_nope, q_pe, kv, kpe, gate, *, batch, seq, tq, rope):
    n, w = q_nope.shape
    tk = kv.shape[1]
    pairs = w // (2 * C_NOPE)
    tps = seq // tq
    hw = C_NOPE + C_V
    qt = pl.BlockSpec((tq, 2 * C_NOPE), lambda b, h, i: (b * tps + i, h))
    specs = [qt, pl.BlockSpec((tq, LANES), lambda b, h, i: (b * tps + i, h)),
             pl.BlockSpec((1, tk, 2 * hw), lambda b, h, i: (b, 0, h)),
             pl.BlockSpec((1, tk, LANES), lambda b, h, i: (b, 0, 0)), qt]
    args = [q_nope, q_pe, kv, kpe, gate]
    if rope:
        tab = pl.BlockSpec((tq, LANES), lambda b, h, i: (i, 0))
        specs += [tab, tab]
        args += list(_rope_tables(seq, C_ROPE))
    return pl.pallas_call(
        functools.partial(_mla_attn_kernel, rope=rope),
        grid=(batch, pairs, tps),
        in_specs=specs,
        out_specs=qt,
        out_shape=jax.ShapeDtypeStruct((n, w), BF16),
        compiler_params=_cparams("parallel", "parallel", "parallel"),
        name="mla_attention",
    )(*args)


def _trunk(x3, cond, mods, wts, caches):
    bsz, seq, d = x3.shape
    n = bsz * seq
    x = x3.reshape(n, d)
    rope = caches is not None
    ab, mla = wts["ab"], wts["mla"]

    def mod_rows(layer):
        m = mods[layer]
        return [m[:, k * d:(k + 1) * d].reshape(-1, 1, d) for k in range(3)]

    shift, scale, gate = mod_rows(0)
    h = rms_norm(x, wts["norm_g"][0], BF16, scale, shift, cond)
    mm_in = lambda w, dt: matmul([h], [w], dt, 1024, 512)
    shifted = mm_in(ab["w_shift"], F32)
    g_a = mm_in(ab["w_ga"], BF16)
    q_b = mm_in(ab["w_q"], F32)
    k_b = mm_in(ab["w_k"], F32)
    v_b = mm_in(ab["w_v"], F32)
    g_b = mm_in(ab["w_gb"], BF16)

    r, v, kk, b, kd, lw, bonus = rwkv_prep(shifted, ab, seq=seq)
    width = r.shape[1]
    if caches is None:
        s0 = jnp.zeros((bsz, 2, width // LANES, LANES, LANES), F32)
    else:
        s0 = _state_to_block_diag(caches[0])
    y, s_fin = rwkv_scan(r, v, kk, b, kd, lw, s0, batch=bsz, seq=seq)
    mixed_a = rwkv_post(y, bonus, g_a, ab["ln_w"], ab["ln_b"])

    q_s, k_f, k_bf, v_bf = gqa_prep(q_b, k_b, v_b, ab["q_g"], ab["k_g"], seq=seq, rope=rope)
    kw = k_bf.shape[1]
    keys = k_bf.reshape(bsz, seq, kw)
    vals = v_bf.reshape(bsz, seq, kw)
    if caches is not None:
        keys = jnp.concatenate([caches[1].reshape(bsz, -1, kw).astype(BF16), keys], axis=1)
        vals = jnp.concatenate([caches[2].reshape(bsz, -1, kw).astype(BF16), vals], axis=1)
    mixed_b = gqa_attention(q_s, keys, vals, g_b, batch=bsz, seq=seq, tq=256)
    x = matmul([mixed_a, mixed_b], [ab["w_out_a"], ab["w_out_b"]], F32, 1024, 512,
               resid=x, gate=gate, cond=cond)

    shift, scale, gate = mod_rows(1)
    h = rms_norm(x, wts["norm_g"][1], BF16, scale, shift, cond)
    qkv_a = matmul([h], [mla["w_qkv"]], F32, 1024, 512)
    kpe = matmul([h], [mla["w_kpe"]], F32, 1024, LANES)
    g_c = matmul([h], [mla["w_gate"]], BF16, 1024, 512)
    qa_n, ckv_f, ckv_b, kpe_b = mla_prep(qkv_a, kpe, mla["q_a_g"], mla["kv_a_g"], seq=seq, rope=rope)
    q_scale = (C_NOPE + C_ROPE) ** -0.5
    q_nope = matmul([qa_n], [mla["w_uq_nope"]], BF16, 1024, 1024, out_scale=q_scale)
    q_pe = matmul([qa_n], [mla["w_uq_pe"]], F32, 1024, 1024, out_scale=q_scale)
    ckv_all = ckv_b.reshape(bsz, seq, C_KV_LORA)
    kpe_all = kpe_b.reshape(bsz, seq, LANES)
    if caches is not None:
        ckv_all = jnp.concatenate([caches[3].astype(BF16), ckv_all], axis=1)
        ctx_kpe = caches[4].astype(BF16)
        kpe_all = jnp.concatenate([jnp.concatenate([ctx_kpe, ctx_kpe], axis=-1), kpe_all], axis=1)
    tk = ckv_all.shape[1]
    kv = matmul([ckv_all.reshape(bsz * tk, C_KV_LORA)], [mla["w_ukv"]], BF16, 1024, 2048)
    o = mla_attention(q_nope, q_pe, kv.reshape(bsz, tk, -1), kpe_all, g_c,
                      batch=bsz, seq=seq, tq=1024 if rope else 256, rope=rope)
    x = matmul([o], [mla["w_out"]], F32, 1024, 256, resid=x, gate=gate, cond=cond)

    y_out = rms_norm(x, wts["final_g"], F32).reshape(bsz, seq, d)
    if caches is not None:
        return y_out, None
    new_state = _state_from_block_diag(s_fin)[:, None]
    new_k = k_f.reshape(bsz, 1, seq, B_KV_HEADS, HEAD_B)
    new_v = v_b.reshape(bsz, 1, seq, B_KV_HEADS, HEAD_B)
    new_ckv = ckv_f.reshape(bsz, 1, seq, C_KV_LORA)
    new_kpe = kpe[:, :C_ROPE].reshape(bsz, 1, seq, C_ROPE)
    return y_out, (new_state, new_k, new_v, new_ckv, new_kpe)


def kernel(x_prompt, x_sample, c, state_rwkv, cache_gqa_k, cache_gqa_v, cache_mla_ckv, cache_mla_kpe, c_ctx, mod_w, mod_b, norm_g, ab_w_in, ab_mix, rwkv_w0, rwkv_w2, rwkv_a0, rwkv_a2, rwkv_k_k, rwkv_k_a, rwkv_r_k, rwkv_ln_w, rwkv_ln_b, gqa_q_g, gqa_k_g, ab_w_out, mla_w_in, mla_q_a_g, mla_kv_a_g, mla_w_uq, mla_w_ukv, mla_w_out, final_g):
    d = x_prompt.shape[-1]
    dec_batch, dec_seq, _ = x_sample.shape
    a_width = rwkv_k_k.shape[-1]
    a_shift = ab_mix.shape[-1]
    b_width = a_width
    b_kv = B_KV_HEADS * HEAD_B
    c_heads = mla_w_out.shape[1] // C_V

    cond = jnp.concatenate([c_ctx[None], c, jnp.zeros((SUBLANES - 1 - dec_batch, d), F32)], axis=0)
    mods = modulation(cond, mod_w, mod_b)

    w_in = ab_w_in[0]
    edges = np.cumsum([0, a_shift, a_width, b_width, b_kv, b_kv, b_width])
    seg = lambda i: w_in[:, edges[i]:edges[i + 1]].astype(BF16)
    ab = {"w_shift": seg(0), "w_ga": seg(1), "w_q": seg(2), "w_k": seg(3), "w_v": seg(4), "w_gb": seg(5),
          "mix": ab_mix[0], "w0": rwkv_w0[0], "w2": rwkv_w2[0], "a0": rwkv_a0[0], "a2": rwkv_a2[0],
          "k_k": rwkv_k_k[0], "k_a": rwkv_k_a[0], "r_k": rwkv_r_k[0], "ln_w": rwkv_ln_w[0],
          "ln_b": rwkv_ln_b[0], "q_g": gqa_q_g[0], "k_g": gqa_k_g[0],
          "w_out_a": ab_w_out[0, :a_width].astype(BF16), "w_out_b": ab_w_out[0, a_width:].astype(BF16)}

    m_in = mla_w_in[0]
    n_qkv = C_Q_LORA + C_KV_LORA
    w_kpe = jnp.pad(m_in[:, n_qkv:n_qkv + C_ROPE], ((0, 0), (0, LANES - C_ROPE))).astype(BF16)
    w_uq = mla_w_uq[0].reshape(C_Q_LORA, c_heads, C_NOPE + C_ROPE)
    mla = {"w_qkv": m_in[:, :n_qkv].astype(BF16), "w_kpe": w_kpe,
           "w_gate": m_in[:, n_qkv + C_ROPE:].astype(BF16),
           "q_a_g": mla_q_a_g[0], "kv_a_g": mla_kv_a_g[0],
           "w_uq_nope": w_uq[:, :, :C_NOPE].reshape(C_Q_LORA, -1).astype(BF16),
           "w_uq_pe": w_uq[:, :, C_NOPE:].reshape(C_Q_LORA, -1).astype(BF16),
           "w_ukv": mla_w_ukv[0].astype(BF16), "w_out": mla_w_out[0].astype(BF16)}
    wts = {"ab": ab, "mla": mla, "norm_g": norm_g, "final_g": final_g}

    y_prompt, ctx_out = _trunk(x_prompt, Cond(0, 0, x_prompt.shape[1]), mods, wts, None)
    caches = (state_rwkv[:, 0], cache_gqa_k[:, 0], cache_gqa_v[:, 0], cache_mla_ckv[:, 0], cache_mla_kpe[:, 0])
    y_sample, _ = _trunk(x_sample, Cond(1, 1, dec_seq), mods, wts, caches)
    return (y_prompt, y_sample) + ctx_out
```

```python
import functools
from typing import NamedTuple

import jax
import jax.numpy as jnp
import numpy as np
from jax import lax
from jax.experimental import pallas as pl
from jax.experimental.pallas import tpu as pltpu

F32 = jnp.float32
BF16 = jnp.bfloat16

LANES = 128
SUBLANES = 8
MXU_DIM = 256
VMEM_LIMIT_BYTES = 56 * 1024 * 1024

NORM_EPS = 1e-6
LOG2_E = 1.4426950408889634
GN_EPS = 64e-5
ROPE_THETA = 10000.0
GRID_W = 64
HEAD_A = 64
HEAD_B = 128
B_KV_HEADS = 4
C_NOPE = 128
C_ROPE = 64
C_V = 128
C_Q_LORA = 1536
C_KV_LORA = 512
LORA = 128
RW_CHUNK = 64

NN = (((1,), (0,)), ((), ()))
NT = (((1,), (1,)), ((), ()))


class Cond(NamedTuple):
    base: int
    per_batch: int
    seq: int

    def row(self, i, tile_rows):
        return self.base + (i * tile_rows // self.seq) * self.per_batch


def _cparams(*sem):
    return pltpu.CompilerParams(dimension_semantics=sem, vmem_limit_bytes=VMEM_LIMIT_BYTES)


def _sigmoid(x):
    return 1.0 / (1.0 + jnp.exp(-x))


def _split(x):
    hi = x.astype(BF16)
    lo = (x - hi.astype(F32)).astype(BF16)
    return hi, lo


def _dot(a, b, dims=NN, passes=1):
    if passes == 1:
        return lax.dot_general(a.astype(BF16), b.astype(BF16), dims, preferred_element_type=F32)
    ah, al = _split(a)
    bh, bl = _split(b)
    out = lax.dot_general(ah, bh, dims, preferred_element_type=F32)
    out = out + lax.dot_general(ah, bl, dims, preferred_element_type=F32)
    return out + lax.dot_general(al, bh, dims, preferred_element_type=F32)


def _group_sum(x, ones_bd):
    hi, lo = _split(x)
    outs = []
    for g in range(x.shape[1] // MXU_DIM):
        sl = slice(g * MXU_DIM, (g + 1) * MXU_DIM)
        outs.append(jnp.dot(hi[:, sl], ones_bd, preferred_element_type=F32)
                    + jnp.dot(lo[:, sl], ones_bd, preferred_element_type=F32))
    return outs[0] if len(outs) == 1 else jnp.concatenate(outs, axis=-1)


def _ones_block_diag(block):
    idx = np.arange(MXU_DIM) // block
    return jnp.asarray(idx[:, None] == idx[None, :], dtype=BF16)


def _rope(x, cos_e, sin_e):
    even = lax.broadcasted_iota(jnp.int32, cos_e.shape, 1) % 2 == 0
    outs = []
    for h in range(x.shape[1] // LANES):
        xh = x[:, h * LANES:(h + 1) * LANES]
        partner = jnp.where(even, pltpu.roll(xh, LANES - 1, 1), pltpu.roll(xh, 1, 1))
        outs.append(xh * cos_e + partner * sin_e)
    return outs[0] if len(outs) == 1 else jnp.concatenate(outs, axis=-1)


def _rope_tables(n_tokens, rot_dim):
    rows = n_tokens // GRID_W
    row = jnp.repeat(jnp.arange(rows, dtype=F32), GRID_W)
    col = jnp.tile(jnp.arange(GRID_W, dtype=F32), rows)
    n_freq = rot_dim // 4
    inv_freq = ROPE_THETA ** (-jnp.arange(n_freq, dtype=F32) / n_freq)
    ang = jnp.concatenate([row[:, None] * inv_freq, col[:, None] * inv_freq], axis=-1)
    cos = jnp.repeat(jnp.cos(ang), 2, axis=-1)
    sin = jnp.repeat(jnp.sin(ang), 2, axis=-1) * jnp.tile(jnp.asarray([-1.0, 1.0], F32), rot_dim // 2)
    reps = LANES // rot_dim
    return jnp.tile(cos, (1, reps)), jnp.tile(sin, (1, reps))


def _mod_kernel(c_ref, w_ref, b_ref, o_ref):
    c = c_ref[...]
    s = c * _sigmoid(c)
    o_ref[0] = jnp.dot(s.astype(BF16), w_ref[0].astype(BF16), preferred_element_type=F32) + b_ref[0]


def modulation(cond, mod_w, mod_b, tn=512):
    depth, d, n = mod_w.shape
    rows = cond.shape[0]
    return pl.pallas_call(
        _mod_kernel,
        grid=(depth, n // tn),
        in_specs=[pl.BlockSpec((rows, d), lambda l, j: (0, 0)),
                  pl.BlockSpec((1, d, tn), lambda l, j: (l, 0, j)),
                  pl.BlockSpec((1, 1, tn), lambda l, j: (l, 0, j))],
        out_specs=pl.BlockSpec((1, rows, tn), lambda l, j: (l, 0, j)),
        out_shape=jax.ShapeDtypeStruct((depth, rows, n), F32),
        compiler_params=_cparams("parallel", "parallel"),
        name="modulation",
    )(cond, mod_w, mod_b.reshape(depth, 1, n))


def _norm_kernel(x_ref, g_ref, *rest, mod):
    x = x_ref[...]
    y = x * lax.rsqrt(jnp.mean(x * x, axis=-1, keepdims=True) + NORM_EPS) * g_ref[...]
    if mod:
        sc_ref, sh_ref, o_ref = rest
        y = y * (1.0 + sc_ref[0]) + sh_ref[0]
    else:
        (o_ref,) = rest
    o_ref[...] = y.astype(o_ref.dtype)


def rms_norm(x, g, out_dtype, scale=None, shift=None, cond=None, tr=256):
    n, d = x.shape
    row = pl.BlockSpec((tr, d), lambda i: (i, 0))
    specs = [row, pl.BlockSpec((1, d), lambda i: (0, 0))]
    args = [x, g.reshape(1, d)]
    if cond is not None:
        cspec = pl.BlockSpec((1, 1, d), lambda i: (cond.row(i, tr), 0, 0))
        specs += [cspec, cspec]
        args += [scale, shift]
    return pl.pallas_call(
        functools.partial(_norm_kernel, mod=cond is not None),
        grid=(n // tr,),
        in_specs=specs,
        out_specs=row,
        out_shape=jax.ShapeDtypeStruct((n, d), out_dtype),
        compiler_params=_cparams("parallel"),
        name="rms_norm",
    )(*args)


def _mm_kernel(*refs, n_pairs, resid, out_scale):
    acc = None
    for p in range(n_pairs):
        part = jnp.dot(refs[p][...], refs[n_pairs + p][...], preferred_element_type=F32)
        acc = part if acc is None else acc + part
    if out_scale != 1.0:
        acc = acc * out_scale
    if resid:
        x_ref, gate_ref, o_ref = refs[2 * n_pairs:]
        acc = x_ref[...] + gate_ref[0] * acc
    else:
        (o_ref,) = refs[2 * n_pairs:]
    o_ref[...] = acc.astype(o_ref.dtype)


def matmul(a_list, w_list, out_dtype, tm, tn, resid=None, gate=None, cond=None, out_scale=1.0):
    m = a_list[0].shape[0]
    n = w_list[0].shape[1]
    specs = [pl.BlockSpec((tm, a.shape[1]), lambda i, j: (i, 0)) for a in a_list]
    specs += [pl.BlockSpec((w.shape[0], tn), lambda i, j: (0, j)) for w in w_list]
    args = list(a_list) + list(w_list)
    tile = pl.BlockSpec((tm, tn), lambda i, j: (i, j))
    if resid is not None:
        specs += [tile, pl.BlockSpec((1, 1, tn), lambda i, j: (cond.row(i, tm), 0, j))]
        args += [resid, gate]
    return pl.pallas_call(
        functools.partial(_mm_kernel, n_pairs=len(a_list), resid=resid is not None, out_scale=out_scale),
        grid=(m // tm, n // tn),
        in_specs=specs,
        out_specs=tile,
        out_shape=jax.ShapeDtypeStruct((m, n), out_dtype),
        compiler_params=_cparams("parallel", "parallel"),
        name="matmul",
    )(*args)


def _rwkv_prep_kernel(x_ref, xp_ref, xn_ref, mix_ref, w0_ref, w2_ref, a0_ref, a2_ref, kk_ref, ka_ref,
                      rk_ref, ones_ref, r_out, v_out, kk_out, b_out, kd_out, lw_out, bonus_out, *,
                      tiles_per_seq, width):
    i = pl.program_id(0)
    x = x_ref[...]
    tr = x.shape[0]
    pos = i % tiles_per_seq
    prev_row = jnp.where(pos == 0, 0.0, xp_ref[SUBLANES - 1:SUBLANES, :])
    next_row = jnp.where(pos == tiles_per_seq - 1, 0.0, xn_ref[0:1, :])
    rid = lax.broadcasted_iota(jnp.int32, (tr, 1), 0)
    prev = jnp.where(rid == 0, prev_row, pltpu.roll(x, 1, 0))
    nxt = jnp.where(rid == tr - 1, next_row, pltpu.roll(x, tr - 1, 0))
    s = x + mix_ref[...] * (0.5 * (prev + nxt) - x)
    r = s[:, :width]
    k = s[:, width:2 * width]
    v = s[:, 2 * width:3 * width]
    ones_bd = ones_ref[...]
    kkr = k * kk_ref[...]
    kk = kkr / jnp.maximum(jnp.sqrt(_group_sum(kkr * kkr, ones_bd)), 1e-12)
    r_out[...] = r
    v_out[...] = v
    kk_out[...] = kk
    rk = r * rk_ref[...]
    bonus = None
    for d in range(2):
        base = 3 * width + 2 * LORA * d
        wl = s[:, base:base + LORA]
        al = s[:, base + LORA:base + 2 * LORA]
        w_pre = w0_ref[d] + jnp.dot(jnp.tanh(wl).astype(BF16), w2_ref[d], preferred_element_type=F32)
        lw_out[d] = -jnp.exp(-0.5) * _sigmoid(w_pre)
        a = _sigmoid(a0_ref[d] + jnp.dot(al.astype(BF16), a2_ref[d], preferred_element_type=F32))
        kd = k * (1.0 + (a - 1.0) * ka_ref[...])
        kd_out[d] = kd
        b_out[d] = kk * a
        bd = _group_sum(rk * kd, ones_bd)
        bonus = bd if bonus is None else bonus + bd
    bonus_out[...] = bonus * v


def rwkv_prep(shifted, p, *, seq, tr=128):
    n, cols = shifted.shape
    width = p["k_k"].shape[-1]
    nb = n // SUBLANES
    per_tile = tr // SUBLANES
    row = lambda c: pl.BlockSpec((1, c), lambda i: (0, 0))
    d3 = lambda a, b: pl.BlockSpec((2, a, b), lambda i: (0, 0, 0))
    shared = pl.BlockSpec((tr, width), lambda i: (i, 0))
    perdir = pl.BlockSpec((2, tr, width), lambda i: (0, i, 0))
    f32 = lambda shape: jax.ShapeDtypeStruct(shape, F32)
    return pl.pallas_call(
        functools.partial(_rwkv_prep_kernel, tiles_per_seq=seq // tr, width=width),
        grid=(n // tr,),
        in_specs=[pl.BlockSpec((tr, cols), lambda i: (i, 0)),
                  pl.BlockSpec((SUBLANES, cols), lambda i: (jnp.maximum(i * per_tile - 1, 0), 0)),
                  pl.BlockSpec((SUBLANES, cols), lambda i: (jnp.minimum((i + 1) * per_tile, nb - 1), 0)),
                  row(cols), d3(1, width), d3(LORA, width), d3(1, width), d3(LORA, width),
                  row(width), row(width), row(width),
                  pl.BlockSpec((MXU_DIM, MXU_DIM), lambda i: (0, 0))],
        out_specs=[shared, shared, shared, perdir, perdir, perdir, shared],
        out_shape=[f32((n, width))] * 3 + [f32((2, n, width))] * 3 + [f32((n, width))],
        compiler_params=_cparams("parallel"),
        name="rwkv_prep",
    )(shifted, shifted, shifted, p["mix"].reshape(1, cols), p["w0"].reshape(2, 1, width),
      p["w2"].astype(BF16), p["a0"].reshape(2, 1, width), p["a2"].astype(BF16),
      p["k_k"].reshape(1, width), p["k_a"].reshape(1, width), p["r_k"].reshape(1, width),
      _ones_block_diag(HEAD_A))


def _each(fn, *lists):
    return [fn(*xs) for xs in zip(*lists)]


def _rwkv_chunks_local(units, tri, strict, incl, eye, p0, xr):
    C = RW_CHUNK
    c2 = 2 * C
    r, v, kk, b, kd, lw = (list(x) for x in zip(*units))

    def cumsum(x):
        hi, lo = _split(x)
        lo2 = (x - hi.astype(F32) - lo.astype(F32)).astype(BF16)
        return (jnp.dot(tri, hi, preferred_element_type=F32) + jnp.dot(tri, lo, preferred_element_type=F32)
                + jnp.dot(tri, lo2, preferred_element_type=F32))

    def stack(x):
        return jnp.concatenate([jnp.where(p0, x, 0.0), jnp.where(p0, 0.0, x)], axis=0)

    L = _each(cumsum, lw)
    tot = _each(lambda x: jnp.sum(x, axis=0, keepdims=True), lw)
    gi = _each(lambda l: jnp.exp(-l), L)
    er = _each(lambda t_, l: jnp.exp(t_ - l), tot, L)
    AV = _each(lambda k_, l, w_: stack(-k_ * jnp.exp(l - w_)), kk, L, lw)
    RT = _each(lambda r_, l: stack(r_ * jnp.exp(l)), r, L)
    VV = _each(stack, v)
    bt = _each(jnp.multiply, b, gi)
    kt = _each(jnp.multiply, kd, gi)
    o1 = _each(lambda a_, r_, b_, k_: _dot(jnp.concatenate([a_, r_], axis=0),
                                           jnp.concatenate([b_, b_, k_, k_], axis=0), NT), AV, RT, bt, kt)
    a_ab = _each(lambda o: jnp.where(strict, o[:c2, :c2], 0.0), o1)
    a_ak = _each(lambda o: jnp.where(strict, o[:c2, c2:], 0.0), o1)
    r_bk = _each(lambda o: jnp.where(jnp.concatenate([incl, incl], axis=1), o[c2:], 0.0), o1)
    z1 = _each(_dot, a_ak, VV)
    T = _each(lambda a_: eye + jnp.where(xr < 2, a_, 0.0), a_ab)
    half = 2
    while half < C:
        E = _each(lambda a_: jnp.where(xr >= half, jnp.where(xr < 2 * half, a_, 0.0), 0.0), a_ab)
        ET = _each(_dot, E, T)
        T = _each(lambda t_, et_: t_ + _dot(t_, et_), T, ET)
        half *= 2
    au = _each(lambda t_, a_, z_: _dot(t_, jnp.concatenate([a_, z_], axis=1)), T, AV, z1)
    Z = _each(lambda a_, v_: jnp.concatenate([a_, jnp.concatenate([jnp.zeros_like(v_), v_], axis=1)], axis=0),
              au, VV)
    ry = _each(_dot, r_bk, Z)
    rp = _each(lambda r_, y_: r_ + y_[:, :c2], RT, ry)
    y0 = _each(lambda y_: y_[:, c2:], ry)
    W = _each(lambda b_, k_, e_: jnp.concatenate([stack(b_ * e_), stack(k_ * e_)], axis=0).T, b, kd, er)
    mn = _each(lambda w_, z_: _dot(w_, z_, NN, 3), W, Z)
    M = _each(lambda t_, m_: jnp.where(eye > 0.5, jnp.exp(t_), 0.0) + m_[:, :c2], tot, mn)
    n0 = _each(lambda m_: m_[:, c2:], mn)
    return rp, y0, M, n0


def _rwkv_kernel(r_ref, v_ref, kk_ref, b_ref, kd_ref, lw_ref, s0_ref, y_ref, st_ref, h_scr, *, G, TB):
    C = RW_CHUNK
    d = pl.program_id(0)
    t = pl.program_id(3)
    nt = pl.num_programs(3)
    nc = TB // C
    rev = d == 1

    @pl.when(t == 0)
    def _():
        h_scr[...] = s0_ref[0, 0]

    c2 = 2 * C
    row = lax.broadcasted_iota(jnp.int32, (c2, c2), 0)
    col = lax.broadcasted_iota(jnp.int32, (c2, c2), 1)
    same = (row >= C) == (col >= C)
    diff = row - col
    diff = jnp.where(rev, -diff, diff)
    diff = jnp.where(same, diff, -1)
    strict = diff > 0
    incl = diff >= 0
    eye = (row == col).astype(F32)
    tr = lax.broadcasted_iota(jnp.int32, (C, C), 0) - lax.broadcasted_iota(jnp.int32, (C, C), 1)
    tr = jnp.where(rev, -tr, tr)
    tri = (tr >= 0).astype(BF16)
    p0 = lax.broadcasted_iota(jnp.int32, (C, LANES), 1) < HEAD_A

    rows = [pl.ds(pl.multiple_of(jnp.where(rev, nc - 1 - ci, ci) * C, C), C) for ci in range(nc)]
    lanes = [slice(p * LANES, (p + 1) * LANES) for p in range(G)]
    units = [(r_ref[rs, ls], v_ref[rs, ls], kk_ref[rs, ls], b_ref[0, rs, ls], kd_ref[0, rs, ls],
              lw_ref[0, rs, ls]) for rs in rows for ls in lanes]
    rp, y0, M, n0 = _rwkv_chunks_local(units, tri, strict, incl, eye, p0, row ^ col)
    H = [h_scr[p] for p in range(G)]
    for ci, rs in enumerate(rows):
        sl = slice(ci * G, (ci + 1) * G)
        Y = _each(lambda a_, h_, c_: _dot(a_, h_) + c_, rp[sl], H, y0[sl])
        H = _each(lambda m_, h_, c_: _dot(m_, h_) + c_, M[sl], H, n0[sl])
        for ls, y in zip(lanes, Y):
            y_ref[0, rs, ls] = y[:C] + y[C:]
    for p in range(G):
        h_scr[p] = H[p]

    @pl.when(t == nt - 1)
    def _():
        st_ref[0, 0] = h_scr[...]


def rwkv_scan(r, v, kk, b, kd, lw, s0, *, batch, seq, G=4, TB=256):
    n, width = r.shape
    assert n == batch * seq and seq % TB == 0 and TB % RW_CHUNK == 0
    pairs = width // LANES
    assert pairs % G == 0
    nt = seq // TB

    def tmap(dd, bb, tt):
        return bb * nt + tt + dd * (nt - 1 - 2 * tt)

    shared = pl.BlockSpec((TB, G * LANES), lambda dd, bb, gg, tt: (tmap(dd, bb, tt), gg))
    perdir = pl.BlockSpec((1, TB, G * LANES), lambda dd, bb, gg, tt: (dd, tmap(dd, bb, tt), gg))
    state = pl.BlockSpec((1, 1, G, LANES, LANES), lambda dd, bb, gg, tt: (bb, dd, gg, 0, 0))
    return pl.pallas_call(
        functools.partial(_rwkv_kernel, G=G, TB=TB),
        grid=(2, batch, pairs // G, nt),
        in_specs=[shared, shared, shared, perdir, perdir, perdir, state],
        out_specs=[perdir, state],
        out_shape=[jax.ShapeDtypeStruct((2, n, width), F32),
                   jax.ShapeDtypeStruct(s0.shape, F32)],
        scratch_shapes=[pltpu.VMEM((G, LANES, LANES), F32)],
        compiler_params=_cparams("parallel", "parallel", "parallel", "arbitrary"),
        name="rwkv_scan",
    )(r, v, kk, b, kd, lw, s0)


def _state_to_block_diag(s):
    bsz, _, heads, n, _ = s.shape
    st = jnp.swapaxes(s, -1, -2).reshape(bsz, 2, heads // 2, 2, n, n)
    z = jnp.zeros_like(st[:, :, :, 0])
    top = jnp.concatenate([st[:, :, :, 0], z], axis=-1)
    bot = jnp.concatenate([z, st[:, :, :, 1]], axis=-1)
    return jnp.concatenate([top, bot], axis=-2)


def _state_from_block_diag(sbd):
    bsz, _, pairs, _, _ = sbd.shape
    n = HEAD_A
    blocks = jnp.stack([sbd[:, :, :, :n, :n], sbd[:, :, :, n:, n:]], axis=3)
    return jnp.swapaxes(blocks.reshape(bsz, 2, 2 * pairs, n, n), -1, -2)


def _rwkv_post_kernel(y_ref, bonus_ref, gate_ref, lnw_ref, lnb_ref, ones_ref, o_ref):
    ones_bd = ones_ref[...]
    y = y_ref[0] + y_ref[1]
    mean = _group_sum(y, ones_bd) * (1.0 / HEAD_A)
    yc = y - mean
    var = _group_sum(yc * yc, ones_bd) * (1.0 / HEAD_A)
    out = yc * lax.rsqrt(var + GN_EPS) * lnw_ref[...] + lnb_ref[...] + bonus_ref[...]
    gt = gate_ref[...].astype(F32)
    o_ref[...] = (out * gt * _sigmoid(gt)).astype(o_ref.dtype)


def rwkv_post(y, bonus, gate, ln_w, ln_b, tr=256):
    _, n, width = y.shape
    tile = pl.BlockSpec((tr, width), lambda i: (i, 0))
    row = pl.BlockSpec((1, width), lambda i: (0, 0))
    return pl.pallas_call(
        _rwkv_post_kernel,
        grid=(n // tr,),
        in_specs=[pl.BlockSpec((2, tr, width), lambda i: (0, i, 0)), tile, tile, row, row,
                  pl.BlockSpec((MXU_DIM, MXU_DIM), lambda i: (0, 0))],
        out_specs=tile,
        out_shape=jax.ShapeDtypeStruct((n, width), BF16),
        compiler_params=_cparams("parallel"),
        name="rwkv_post",
    )(y, bonus, gate, ln_w.reshape(1, width), ln_b.reshape(1, width), _ones_block_diag(HEAD_A))


def _gqa_prep_kernel(q_ref, k_ref, v_ref, qg_ref, kg_ref, ones_ref, *rest, rope, scale):
    if rope:
        cos_ref, sin_ref, qo_ref, kf_ref, kb_ref, vb_ref = rest
    else:
        qo_ref, kf_ref, kb_ref, vb_ref = rest
    ones_bd = ones_ref[...]

    def norm(x, g):
        ms = _group_sum(x * x, ones_bd) * (1.0 / HEAD_B)
        return x * lax.rsqrt(ms + NORM_EPS) * g

    q = norm(q_ref[...], qg_ref[...])
    k = norm(k_ref[...], kg_ref[...])
    if rope:
        q = _rope(q, cos_ref[...], sin_ref[...])
        k = _rope(k, cos_ref[...], sin_ref[...])
    qo_ref[...] = (q * scale).astype(BF16)
    kf_ref[...] = k
    kb_ref[...] = k.astype(BF16)
    vb_ref[...] = v_ref[...].astype(BF16)


def gqa_prep(q, k, v, q_g, k_g, *, seq, rope, tr=256):
    n, qw = q.shape
    kw = k.shape[1]
    tps = seq // tr
    qt = pl.BlockSpec((tr, qw), lambda i: (i, 0))
    kt = pl.BlockSpec((tr, kw), lambda i: (i, 0))
    specs = [qt, kt, kt, pl.BlockSpec((1, qw), lambda i: (0, 0)), pl.BlockSpec((1, kw), lambda i: (0, 0)),
             pl.BlockSpec((MXU_DIM, MXU_DIM), lambda i: (0, 0))]
    args = [q, k, v, jnp.tile(q_g, qw // HEAD_B).reshape(1, qw), jnp.tile(k_g, kw // HEAD_B).reshape(1, kw),
            _ones_block_diag(HEAD_B)]
    if rope:
        tab = pl.BlockSpec((tr, LANES), lambda i: (i % tps, 0))
        specs += [tab, tab]
        args += list(_rope_tables(seq, HEAD_B))
    return pl.pallas_call(
        functools.partial(_gqa_prep_kernel, rope=rope, scale=HEAD_B ** -0.5 * LOG2_E),
        grid=(n // tr,),
        in_specs=specs,
        out_specs=[qt, kt, kt, kt],
        out_shape=[jax.ShapeDtypeStruct((n, qw), BF16), jax.ShapeDtypeStruct((n, kw), F32),
                   jax.ShapeDtypeStruct((n, kw), BF16), jax.ShapeDtypeStruct((n, kw), BF16)],
        compiler_params=_cparams("parallel"),
        name="gqa_prep",
    )(*args)


def _attend(q, k, v):
    s = lax.dot_general(k, q, NT, preferred_element_type=F32)
    m = jnp.max(s, axis=0, keepdims=True)
    p = jnp.exp2(s - m)
    l = jnp.sum(p, axis=0, keepdims=True)
    o = jnp.dot(v.T, p.astype(BF16), preferred_element_type=F32)
    return (o / l).T


def _gqa_attn_kernel(q_ref, k_ref, v_ref, g_ref, o_ref, *, group):
    q = q_ref[...]
    tq = q.shape[0]
    q4 = jnp.concatenate([q[:, g * HEAD_B:(g + 1) * HEAD_B] for g in range(group)], axis=0)
    o = _attend(q4, k_ref[0], v_ref[0])
    o = jnp.concatenate([o[g * tq:(g + 1) * tq] for g in range(group)], axis=-1)
    gt = g_ref[...].astype(F32)
    o_ref[...] = (o * gt * _sigmoid(gt)).astype(o_ref.dtype)


def gqa_attention(q, keys, vals, gate, *, batch, seq, tq):
    n, qw = q.shape
    tk = keys.shape[1]
    group = qw // HEAD_B // B_KV_HEADS
    gw = group * HEAD_B
    tps = seq // tq
    qt = pl.BlockSpec((tq, gw), lambda b, h, i: (b * tps + i, h))
    kvt = pl.BlockSpec((1, tk, HEAD_B), lambda b, h, i: (b, 0, h))
    return pl.pallas_call(
        functools.partial(_gqa_attn_kernel, group=group),
        grid=(batch, B_KV_HEADS, tps),
        in_specs=[qt, kvt, kvt, qt],
        out_specs=qt,
        out_shape=jax.ShapeDtypeStruct((n, qw), BF16),
        compiler_params=_cparams("parallel", "parallel", "parallel"),
        name="gqa_attention",
    )(q, keys, vals, gate)


def _mla_prep_kernel(qkv_ref, kpe_ref, qg_ref, kvg_ref, *rest, rope):
    if rope:
        cos_ref, sin_ref, qa_ref, ckv_f_ref, ckv_b_ref, kpe_b_ref = rest
    else:
        qa_ref, ckv_f_ref, ckv_b_ref, kpe_b_ref = rest

    def norm(x, g):
        return x * lax.rsqrt(jnp.mean(x * x, axis=-1, keepdims=True) + NORM_EPS) * g

    x = qkv_ref[...]
    qa_ref[...] = norm(x[:, :C_Q_LORA], qg_ref[...]).astype(BF16)
    ckv = norm(x[:, C_Q_LORA:], kvg_ref[...])
    ckv_f_ref[...] = ckv
    ckv_b_ref[...] = ckv.astype(BF16)
    kpe = kpe_ref[...]
    if rope:
        kpe = _rope(kpe, cos_ref[...], sin_ref[...])
    kpe_b_ref[...] = (kpe + pltpu.roll(kpe, C_ROPE, 1)).astype(BF16)


def mla_prep(qkv_a, kpe, q_g, kv_g, *, seq, rope, tr=256):
    n, w = qkv_a.shape
    tps = seq // tr
    tile = lambda c: pl.BlockSpec((tr, c), lambda i: (i, 0))
    row = lambda c: pl.BlockSpec((1, c), lambda i: (0, 0))
    specs = [tile(w), tile(LANES), row(C_Q_LORA), row(C_KV_LORA)]
    args = [qkv_a, kpe, q_g.reshape(1, C_Q_LORA), kv_g.reshape(1, C_KV_LORA)]
    if rope:
        tab = pl.BlockSpec((tr, LANES), lambda i: (i % tps, 0))
        specs += [tab, tab]
        args += list(_rope_tables(seq, C_ROPE))
    return pl.pallas_call(
        functools.partial(_mla_prep_kernel, rope=rope),
        grid=(n // tr,),
        in_specs=specs,
        out_specs=[tile(C_Q_LORA), tile(C_KV_LORA), tile(C_KV_LORA), tile(LANES)],
        out_shape=[jax.ShapeDtypeStruct((n, C_Q_LORA), BF16), jax.ShapeDtypeStruct((n, C_KV_LORA), F32),
                   jax.ShapeDtypeStruct((n, C_KV_LORA), BF16), jax.ShapeDtypeStruct((n, LANES), BF16)],
        compiler_params=_cparams("parallel"),
        name="mla_prep",
    )(*args)


def _mla_attn_kernel(qn_ref, qp_ref, kv_ref, kp_ref, g_ref, *rest, rope):
    if rope:
        cos_ref, sin_ref, o_ref = rest
    else:
        (o_ref,) = rest
    qp = qp_ref[...]
    if rope:
        qp = _rope(qp, cos_ref[...], sin_ref[...])
    first = lax.broadcasted_iota(jnp.int32, qp.shape, 1) < C_ROPE
    kp = kp_ref[0]
    hw = C_NOPE + C_V
    outs = []
    for j in range(2):
        qpj = jnp.where(first if j == 0 else jnp.logical_not(first), qp, 0.0).astype(BF16)
        qj = jnp.concatenate([qn_ref[:, j * C_NOPE:(j + 1) * C_NOPE], qpj], axis=-1)
        kj = jnp.concatenate([kv_ref[0, :, j * hw:j * hw + C_NOPE], kp], axis=-1)
        outs.append(_attend(qj, kj, kv_ref[0, :, j * hw + C_NOPE:(j + 1) * hw]))
    o = jnp.concatenate(outs, axis=-1)
    gt = g_ref[...].astype(F32)
    o_ref[...] = (o * gt * _sigmoid(gt)).astype(o_ref.dtype)


def mla_attention(q_nope, q_pe, kv, kpe, gate, *, batch, seq, tq, rope):
    n, w = q_nope.shape
    tk = kv.shape[1]
    pairs = w // (2 * C_NOPE)
    tps = seq // tq
    hw = C_NOPE + C_V
    qt = pl.BlockSpec((tq, 2 * C_NOPE), lambda b, h, i: (b * tps + i, h))
    specs = [qt, pl.BlockSpec((tq, LANES), lambda b, h, i: (b * tps + i, h)),
             pl.BlockSpec((1, tk, 2 * hw), lambda b, h, i: (b, 0, h)),
             pl.BlockSpec((1, tk, LANES), lambda b, h, i: (b, 0, 0)), qt]
    args = [q_nope, q_pe, kv, kpe, gate]
    if rope:
        tab = pl.BlockSpec((tq, LANES), lambda b, h, i: (i, 0))
        specs += [tab, tab]
        args += list(_rope_tables(seq, C_ROPE))
    return pl.pallas_call(
        functools.partial(_mla_attn_kernel, rope=rope),
        grid=(batch, pairs, tps),
        in_specs=specs,
        out_specs=qt,
        out_shape=jax.ShapeDtypeStruct((n, w), BF16),
        compiler_params=_cparams("parallel", "parallel", "parallel"),
        name="mla_attention",
    )(*args)


def _trunk(x3, cond, mods, wts, caches):
    bsz, seq, d = x3.shape
    n = bsz * seq
    x = x3.reshape(n, d)
    rope = caches is not None
    ab, mla = wts["ab"], wts["mla"]

    def mod_rows(layer):
        m = mods[layer]
        return [m[:, k * d:(k + 1) * d].reshape(-1, 1, d) for k in range(3)]

    shift, scale, gate = mod_rows(0)
    h = rms_norm(x, wts["norm_g"][0], BF16, scale, shift, cond)
    mm_in = lambda w, dt: matmul([h], [w], dt, 1024, 512)
    shifted = mm_in(ab["w_shift"], F32)
    g_a = mm_in(ab["w_ga"], BF16)
    q_b = mm_in(ab["w_q"], F32)
    k_b = mm_in(ab["w_k"], F32)
    v_b = mm_in(ab["w_v"], F32)
    g_b = mm_in(ab["w_gb"], BF16)

    r, v, kk, b, kd, lw, bonus = rwkv_prep(shifted, ab, seq=seq)
    width = r.shape[1]
    if caches is None:
        s0 = jnp.zeros((bsz, 2, width // LANES, LANES, LANES), F32)
    else:
        s0 = _state_to_block_diag(caches[0])
    y, s_fin = rwkv_scan(r, v, kk, b, kd, lw, s0, batch=bsz, seq=seq)
    mixed_a = rwkv_post(y, bonus, g_a, ab["ln_w"], ab["ln_b"])

    q_s, k_f, k_bf, v_bf = gqa_prep(q_b, k_b, v_b, ab["q_g"], ab["k_g"], seq=seq, rope=rope)
    kw = k_bf.shape[1]
    keys = k_bf.reshape(bsz, seq, kw)
    vals = v_bf.reshape(bsz, seq, kw)
    if caches is not None:
        keys = jnp.concatenate([caches[1].reshape(bsz, -1, kw).astype(BF16), keys], axis=1)
        vals = jnp.concatenate([caches[2].reshape(bsz, -1, kw).astype(BF16), vals], axis=1)
    mixed_b = gqa_attention(q_s, keys, vals, g_b, batch=bsz, seq=seq, tq=256)
    x = matmul([mixed_a, mixed_b], [ab["w_out_a"], ab["w_out_b"]], F32, 1024, 512,
               resid=x, gate=gate, cond=cond)

    shift, scale, gate = mod_rows(1)
    h = rms_norm(x, wts["norm_g"][1], BF16, scale, shift, cond)
    qkv_a = matmul([h], [mla["w_qkv"]], F32, 1024, 512)
    kpe = matmul([h], [mla["w_kpe"]], F32, 1024, LANES)
    g_c = matmul([h], [mla["w_gate"]], BF16, 1024, 512)
    qa_n, ckv_f, ckv_b, kpe_b = mla_prep(qkv_a, kpe, mla["q_a_g"], mla["kv_a_g"], seq=seq, rope=rope)
    q_scale = (C_NOPE + C_ROPE) ** -0.5 * LOG2_E
    q_nope = matmul([qa_n], [mla["w_uq_nope"]], BF16, 1024, 1024, out_scale=q_scale)
    q_pe = matmul([qa_n], [mla["w_uq_pe"]], F32, 1024, 1024, out_scale=q_scale)
    ckv_all = ckv_b.reshape(bsz, seq, C_KV_LORA)
    kpe_all = kpe_b.reshape(bsz, seq, LANES)
    if caches is not None:
        ckv_all = jnp.concatenate([caches[3].astype(BF16), ckv_all], axis=1)
        ctx_kpe = caches[4].astype(BF16)
        kpe_all = jnp.concatenate([jnp.concatenate([ctx_kpe, ctx_kpe], axis=-1), kpe_all], axis=1)
    tk = ckv_all.shape[1]
    kv = matmul([ckv_all.reshape(bsz * tk, C_KV_LORA)], [mla["w_ukv"]], BF16, 1024, 2048)
    o = mla_attention(q_nope, q_pe, kv.reshape(bsz, tk, -1), kpe_all, g_c,
                      batch=bsz, seq=seq, tq=1024 if rope else 256, rope=rope)
    x = matmul([o], [mla["w_out"]], F32, 1024, 256, resid=x, gate=gate, cond=cond)

    y_out = rms_norm(x, wts["final_g"], F32).reshape(bsz, seq, d)
    if caches is not None:
        return y_out, None
    new_state = _state_from_block_diag(s_fin)[:, None]
    new_k = k_f.reshape(bsz, 1, seq, B_KV_HEADS, HEAD_B)
    new_v = v_b.reshape(bsz, 1, seq, B_KV_HEADS, HEAD_B)
    new_ckv = ckv_f.reshape(bsz, 1, seq, C_KV_LORA)
    new_kpe = kpe[:, :C_ROPE].reshape(bsz, 1, seq, C_ROPE)
    return y_out, (new_state, new_k, new_v, new_ckv, new_kpe)


def kernel(x_prompt, x_sample, c, state_rwkv, cache_gqa_k, cache_gqa_v, cache_mla_ckv, cache_mla_kpe, c_ctx, mod_w, mod_b, norm_g, ab_w_in, ab_mix, rwkv_w0, rwkv_w2, rwkv_a0, rwkv_a2, rwkv_k_k, rwkv_k_a, rwkv_r_k, rwkv_ln_w, rwkv_ln_b, gqa_q_g, gqa_k_g, ab_w_out, mla_w_in, mla_q_a_g, mla_kv_a_g, mla_w_uq, mla_w_ukv, mla_w_out, final_g):
    d = x_prompt.shape[-1]
    dec_batch, dec_seq, _ = x_sample.shape
    a_width = rwkv_k_k.shape[-1]
    a_shift = ab_mix.shape[-1]
    b_width = a_width
    b_kv = B_KV_HEADS * HEAD_B
    c_heads = mla_w_out.shape[1] // C_V

    cond = jnp.concatenate([c_ctx[None], c, jnp.zeros((SUBLANES - 1 - dec_batch, d), F32)], axis=0)
    mods = modulation(cond, mod_w, mod_b)

    w_in = ab_w_in[0]
    edges = np.cumsum([0, a_shift, a_width, b_width, b_kv, b_kv, b_width])
    seg = lambda i: w_in[:, edges[i]:edges[i + 1]].astype(BF16)
    ab = {"w_shift": seg(0), "w_ga": seg(1), "w_q": seg(2), "w_k": seg(3), "w_v": seg(4), "w_gb": seg(5),
          "mix": ab_mix[0], "w0": rwkv_w0[0], "w2": rwkv_w2[0], "a0": rwkv_a0[0], "a2": rwkv_a2[0],
          "k_k": rwkv_k_k[0], "k_a": rwkv_k_a[0], "r_k": rwkv_r_k[0], "ln_w": rwkv_ln_w[0],
          "ln_b": rwkv_ln_b[0], "q_g": gqa_q_g[0], "k_g": gqa_k_g[0],
          "w_out_a": ab_w_out[0, :a_width].astype(BF16), "w_out_b": ab_w_out[0, a_width:].astype(BF16)}

    m_in = mla_w_in[0]
    n_qkv = C_Q_LORA + C_KV_LORA
    w_kpe = jnp.pad(m_in[:, n_qkv:n_qkv + C_ROPE], ((0, 0), (0, LANES - C_ROPE))).astype(BF16)
    w_uq = mla_w_uq[0].reshape(C_Q_LORA, c_heads, C_NOPE + C_ROPE)
    mla = {"w_qkv": m_in[:, :n_qkv].astype(BF16), "w_kpe": w_kpe,
           "w_gate": m_in[:, n_qkv + C_ROPE:].astype(BF16),
           "q_a_g": mla_q_a_g[0], "kv_a_g": mla_kv_a_g[0],
           "w_uq_nope": w_uq[:, :, :C_NOPE].reshape(C_Q_LORA, -1).astype(BF16),
           "w_uq_pe": w_uq[:, :, C_NOPE:].reshape(C_Q_LORA, -1).astype(BF16),
           "w_ukv": mla_w_ukv[0].astype(BF16), "w_out": mla_w_out[0].astype(BF16)}
    wts = {"ab": ab, "mla": mla, "norm_g": norm_g, "final_g": final_g}

    y_prompt, ctx_out = _trunk(x_prompt, Cond(0, 0, x_prompt.shape[1]), mods, wts, None)
    caches = (state_rwkv[:, 0], cache_gqa_k[:, 0], cache_gqa_v[:, 0], cache_mla_ckv[:, 0], cache_mla_kpe[:, 0])
    y_sample, _ = _trunk(x_sample, Cond(1, 1, dec_seq), mods, wts, caches)
    return (y_prompt, y_sample) + ctx_out
```
